```python
import math
import jax
import jax.numpy as jnp
from jax import lax
import numpy as np

D_MODEL = 1024
BATCH = 8
SEQ = 4096
DEPTH = 1

CTX_LEN = 256
GRID_W = 64
A_HEADS = 8
A_DH = 64
A_DV = 2 * A_DH
A_WIDTH = A_HEADS * A_DV
H_HEADS = 8
H_DK = 128
H_DV = 128
H_WIDTH = H_HEADS * H_DV
CHUNK = 64
Q_BLOCK = 128
ROPE_BASE = 10000.0
EPS = 1e-6

IN_COLS = (
    ("attn_k", A_HEADS * 2 * A_DH),
    ("attn_v", A_WIDTH),
    ("hgrn_i", H_WIDTH),
    ("hgrn_f_fwd", H_HEADS * H_DK),
    ("hgrn_f_bwd", H_HEADS * H_DK),
    ("attn_q", A_HEADS * 2 * A_DH),
    ("hgrn_q", H_HEADS * H_DK),
    ("attn_z", A_WIDTH),
    ("hgrn_z", H_WIDTH),
    ("merge_gate", 2 * D_MODEL),
)
IN_WIDTH = (4 * A_HEADS * A_DH + A_WIDTH + H_WIDTH + 3 * H_HEADS * H_DK
            + A_WIDTH + H_WIDTH + 2 * D_MODEL)

kernel_name = "hybrid_diffattn_hgrn2_dit_layer"


def _rms(u, gain):
    uf = u.astype(jnp.float32)
    y = uf * lax.rsqrt(jnp.mean(uf * uf, axis=-1, keepdims=True) + EPS)
    return (y * gain.astype(jnp.float32)).astype(u.dtype)


def _cols(w, name):
    start = 0
    for n, width in IN_COLS:
        if n == name:
            return w[:, start:start + width]
        start += width
    raise KeyError(name)


def _axial_rope(n_tok):
    rows = n_tok // GRID_W
    row = jnp.repeat(jnp.arange(rows, dtype=jnp.float32), GRID_W)
    col = jnp.tile(jnp.arange(GRID_W, dtype=jnp.float32), rows)
    half = A_DH // 2
    inv = ROPE_BASE ** (-jnp.arange(0, half, 2, dtype=jnp.float32) / half)
    ar = row[:, None] * inv[None, :]
    ac = col[:, None] * inv[None, :]
    ang = jnp.concatenate([ar, ar, ac, ac], axis=-1)
    return jnp.cos(ang), jnp.sin(ang)


def _rotate_half_axial(u):
    ur, uc = jnp.split(u, 2, axis=-1)
    r1, r2 = jnp.split(ur, 2, axis=-1)
    c1, c2 = jnp.split(uc, 2, axis=-1)
    return jnp.concatenate([-r2, r1, -c2, c1], axis=-1)


def _apply_rope(u, cos, sin):
    return (u * cos + _rotate_half_axial(u) * sin).astype(u.dtype)


def _qk_heads(p, gain):
    B, T, _ = p.shape
    u = _rms(p.reshape(B, T, A_HEADS, 2, A_DH), gain)
    return u.transpose(0, 2, 3, 1, 4)


def _v_heads(p):
    B, T, _ = p.shape
    return p.reshape(B, T, A_HEADS, A_DV).transpose(0, 2, 1, 3)


def _diff_attention(q, k, v, lam):
    B, H, _, T, dh = q.shape
    nb = T // Q_BLOCK
    qb = q.reshape(B, H, 2, nb, Q_BLOCK, dh).transpose(3, 0, 1, 2, 4, 5)

    def block(qblk):
        s = jnp.einsum("bhmqd,bhmkd->bhmqk", qblk, k).astype(jnp.float32)
        p = jax.nn.softmax(s, axis=-1)
        w = p[:, :, 0] - lam * p[:, :, 1]
        return jnp.einsum("bhqk,bhkd->bhqd", w.astype(v.dtype), v)

    o = lax.map(block, qb)
    return o.transpose(1, 0, 3, 2, 4).reshape(B, T, H, A_DV)


def _lower_bound(p, l):
    return jnp.cumsum(jax.nn.softmax(p.astype(jnp.float32), axis=0), axis=0)[l]


def _log_forget(a, lb):
    B, T, _ = a.shape
    f = lb + (1.0 - lb) * jax.nn.sigmoid(a.astype(jnp.float32))
    return jnp.log(f).reshape(B, T, H_HEADS, H_DK)


def _hgrn2_scan(q, log_f, v, s0):
    B, T, H, _ = log_f.shape
    n = T // CHUNK

    def chunks(u):
        return u.astype(jnp.float32).reshape(B, n, CHUNK, H, u.shape[-1]).transpose(1, 0, 3, 2, 4)

    k = -jnp.expm1(log_f.astype(jnp.float32))
    xs = (chunks(log_f), chunks(k), chunks(v))
    if q is not None:
        xs = xs + (chunks(q),)
    mask = jnp.tril(jnp.ones((CHUNK, CHUNK), dtype=bool))[:, :, None]

    def step(S, inp):
        g, kk, vv = inp[:3]
        b = jnp.cumsum(g, axis=2)
        b_end = b[:, :, -1:, :]
        S_new = (jnp.exp(b_end[:, :, 0, :])[..., None] * S
                 + jnp.einsum("bhsk,bhsv->bhkv", kk * jnp.exp(b_end - b), vv))
        if q is None:
            return S_new, None
        qq = inp[3]
        decay = jnp.exp(jnp.where(mask, b[:, :, :, None, :] - b[:, :, None, :, :], -jnp.inf))
        scores = jnp.einsum("bhtk,bhtsk,bhsk->bhts", qq, decay, kk)
        o = (jnp.einsum("bhts,bhsv->bhtv", scores, vv)
             + jnp.einsum("bhtk,bhkv->bhtv", qq * jnp.exp(b), S))
        return S_new, o

    S_fin, o = lax.scan(step, s0, xs)
    if q is None:
        return None, S_fin
    return o.transpose(1, 0, 3, 2, 4).reshape(B, T, H, H_DV), S_fin


def _hgrn2_bidir(q_lat, f_lat, b_lat, v_lat, q_ctx, f_ctx, b_ctx, v_ctx):
    B = v_lat.shape[0]
    s0 = jnp.zeros((B, H_HEADS, H_DK, H_DV), jnp.float32)
    flip = lambda u: jnp.flip(u, axis=1)
    o_cf, s_f = _hgrn2_scan(q_ctx, f_ctx, v_ctx, s0)
    o_cb, s_b = _hgrn2_scan(None if q_ctx is None else flip(q_ctx), flip(b_ctx), flip(v_ctx), s0)
    o_lf, _ = _hgrn2_scan(q_lat, f_lat, v_lat, s_f)
    o_lb, _ = _hgrn2_scan(flip(q_lat), flip(b_lat), flip(v_lat), s_b)
    o_ctx = None if q_ctx is None else o_cf + flip(o_cb)
    return o_lf + flip(o_lb), o_ctx


def _merge(h, W, o_attn, o_hgrn, lam_init, subln_g, hgrn_g, w_ba, w_bh, w_o):
    B, T, _ = h.shape
    ya = (_rms(o_attn, subln_g) * (1.0 - lam_init)).reshape(B, T, A_WIDTH) * jax.nn.silu(h @ _cols(W, "attn_z"))
    yh = _rms(o_hgrn.astype(h.dtype), hgrn_g).reshape(B, T, H_WIDTH) * jax.nn.silu(h @ _cols(W, "hgrn_z"))
    g_a, g_h = jnp.split(jax.nn.sigmoid(h @ _cols(W, "merge_gate")), 2, axis=-1)
    return (g_a * (ya @ w_ba) + g_h * (yh @ w_bh)) @ w_o


def setup_inputs(seed: int = 0) -> dict:
    key = jax.random.key(seed)
    ks = jax.random.split(key, 24)
    D = D_MODEL

    def nrm(k, shape, s):
        return jax.random.normal(k, shape, jnp.float32) * s

    return {
        "x": nrm(ks[0], (BATCH, SEQ, D), 1.0),
        "c": nrm(ks[1], (BATCH, D), 1.0),
        "ctx": nrm(ks[2], (BATCH, CTX_LEN, D), 1.0),
        "c_ctx": nrm(ks[3], (D,), 1.0),
        "w_mod": nrm(ks[4], (DEPTH, D, 3 * D), D ** -0.5),
        "b_mod": nrm(ks[5], (DEPTH, 3 * D), 0.02),
        "norm_gain": 1.0 + nrm(ks[6], (DEPTH, D), 0.02),
        "w_in": nrm(ks[7], (DEPTH, D, IN_WIDTH), D ** -0.5),
        "q_norm_gain": 1.0 + nrm(ks[8], (DEPTH, A_DH), 0.02),
        "k_norm_gain": 1.0 + nrm(ks[9], (DEPTH, A_DH), 0.02),
        "lambda_q1": nrm(ks[10], (DEPTH, A_DH), 0.1),
        "lambda_k1": nrm(ks[11], (DEPTH, A_DH), 0.1),
        "lambda_q2": nrm(ks[12], (DEPTH, A_DH), 0.1),
        "lambda_k2": nrm(ks[13], (DEPTH, A_DH), 0.1),
        "subln_gain": 1.0 + nrm(ks[14], (DEPTH, A_DV), 0.02),
        "hgrn_lb_fwd": nrm(ks[15], (DEPTH + 1, H_HEADS * H_DK), 0.5),
        "hgrn_lb_bwd": nrm(ks[16], (DEPTH + 1, H_HEADS * H_DK), 0.5),
        "hgrn_norm_gain": 1.0 + nrm(ks[17], (DEPTH, H_DV), 0.02),
        "w_br_attn": nrm(ks[18], (DEPTH, A_WIDTH, D), A_WIDTH ** -0.5),
        "w_br_hgrn": nrm(ks[19], (DEPTH, H_WIDTH, D), H_WIDTH ** -0.5),
        "w_out": nrm(ks[20], (DEPTH, D, D), D ** -0.5),
    }


def reference(x, c, ctx, c_ctx, w_mod, b_mod, norm_gain, w_in, q_norm_gain, k_norm_gain,
              lambda_q1, lambda_k1, lambda_q2, lambda_k2, subln_gain, hgrn_lb_fwd, hgrn_lb_bwd,
              hgrn_norm_gain, w_br_attn, w_br_hgrn, w_out):
    B, T, _ = x.shape
    Lc = ctx.shape[1]
    cos, sin = _axial_rope(T)
    q_scale = A_DH ** -0.5
    f32 = jnp.float32
    for l in range(DEPTH):
        last = l == DEPTH - 1
        W = w_in[l]
        lam_init = 0.8 - 0.6 * math.exp(-0.3 * l)
        mod = jax.nn.silu(c) @ w_mod[l] + b_mod[l]
        shift, scale, gate = jnp.split(mod[:, None, :], 3, axis=-1)
        mod_c = jax.nn.silu(c_ctx) @ w_mod[l] + b_mod[l]
        shift_c, scale_c, gate_c = jnp.split(mod_c, 3, axis=-1)
        h = _rms(x, norm_gain[l]) * (1.0 + scale) + shift
        hc = _rms(ctx, norm_gain[l]) * (1.0 + scale_c) + shift_c

        lam = (jnp.exp(jnp.sum(lambda_q1[l].astype(f32) * lambda_k1[l].astype(f32)))
               - jnp.exp(jnp.sum(lambda_q2[l].astype(f32) * lambda_k2[l].astype(f32))) + lam_init)
        k_c = _qk_heads(hc @ _cols(W, "attn_k"), k_norm_gain[l])
        v_c = _v_heads(hc @ _cols(W, "attn_v"))
        q_l = _apply_rope(_qk_heads(h @ _cols(W, "attn_q"), q_norm_gain[l]), cos, sin) * q_scale
        k_l = _apply_rope(_qk_heads(h @ _cols(W, "attn_k"), k_norm_gain[l]), cos, sin)
        v_l = _v_heads(h @ _cols(W, "attn_v"))
        o_a = _diff_attention(q_l, jnp.concatenate([k_c, k_l], axis=3),
                              jnp.concatenate([v_c, v_l], axis=2), lam)

        lb_f = _lower_bound(hgrn_lb_fwd, l)
        lb_b = _lower_bound(hgrn_lb_bwd, l)
        q_h = jax.nn.silu(h @ _cols(W, "hgrn_q")).reshape(B, T, H_HEADS, H_DK)
        f_l = _log_forget(h @ _cols(W, "hgrn_f_fwd"), lb_f)
        b_l = _log_forget(h @ _cols(W, "hgrn_f_bwd"), lb_b)
        v_hl = (h @ _cols(W, "hgrn_i")).reshape(B, T, H_HEADS, H_DV)
        f_c = _log_forget(hc @ _cols(W, "hgrn_f_fwd"), lb_f)
        b_c = _log_forget(hc @ _cols(W, "hgrn_f_bwd"), lb_b)
        v_hc = (hc @ _cols(W, "hgrn_i")).reshape(B, Lc, H_HEADS, H_DV)
        q_hc = None if last else jax.nn.silu(hc @ _cols(W, "hgrn_q")).reshape(B, Lc, H_HEADS, H_DK)
        o_h, o_hc = _hgrn2_bidir(q_h, f_l, b_l, v_hl, q_hc, f_c, b_c, v_hc)

        y = _merge(h, W, o_a, o_h, lam_init, subln_gain[l], hgrn_norm_gain[l],
                   w_br_attn[l], w_br_hgrn[l], w_out[l])
        if not last:
            q_c = _qk_heads(hc @ _cols(W, "attn_q"), q_norm_gain[l]) * q_scale
            o_ac = _diff_attention(q_c, k_c, v_c, lam)
            ctx = ctx + gate_c * _merge(hc, W, o_ac, o_hc, lam_init, subln_gain[l], hgrn_norm_gain[l],
                                        w_br_attn[l], w_br_hgrn[l], w_out[l])
        x = x + gate * y
    return x
```

```python
import functools
import math

import jax
import jax.numpy as jnp
import numpy as np
from jax import lax
from jax.experimental import pallas as pl
from jax.experimental.pallas import tpu as pltpu

F32 = jnp.float32
BF16 = jnp.bfloat16

HEADS = 8
HEAD_W = 128
MAP_W = 64
GRID_W = 64
ROPE_BASE = 10000.0
EPS = 1e-6
LAM_INIT = 0.8 - 0.6 * math.exp(-0.3 * 0)

PROJ_ROWS = 256
ATTN_ROWS = 256
MERGE_ROWS = 512
CHUNK = 64
VMEM_LIMIT = 56 * 1024 * 1024

G_ATTN_K, G_ATTN_V, G_HGRN_I, G_F_FWD, G_F_BWD, G_ATTN_Q, G_HGRN_Q, G_ATTN_Z, G_HGRN_Z, G_MERGE = range(10)

_NT = (((1,), (1,)), ((), ()))


def _sigmoid(a):
    return 1.0 / (1.0 + jnp.exp(-a))


def _mod_kernel(c_ref, w_ref, b_ref, o_ref):
    a = c_ref[...]
    s = a * _sigmoid(a)
    o_ref[...] = jnp.dot(s, w_ref[...], precision=lax.Precision.HIGHEST,
                         preferred_element_type=F32) + b_ref[...]


def _mod_call(cc, w_mod, b_mod):
    rows, d = cc.shape
    n = w_mod.shape[1]
    bn = 1024
    return pl.pallas_call(
        _mod_kernel,
        grid=(n // bn,),
        in_specs=[pl.BlockSpec((rows, d), lambda j: (0, 0)),
                  pl.BlockSpec((d, bn), lambda j: (0, j)),
                  pl.BlockSpec((1, bn), lambda j: (0, j))],
        out_specs=pl.BlockSpec((rows, bn), lambda j: (0, j)),
        out_shape=jax.ShapeDtypeStruct((rows, n), F32),
        name="mod",
    )(cc, w_mod, b_mod)


def _proj_kernel(x_ref, ctx_ref, mod_ref, ng_ref, w_ref, qg_ref, kg_ref, cos_ref, sa_ref, sb_ref,
                 lbf_ref, lbb_ref, gsum_ref,
                 k_out, v_out, hv_out, gf_out, gb_out, q_out, hq_out, za_out, zh_out, mg_out,
                 *, n_ctx_tiles, ctx_row, d):
    b = pl.program_id(0)
    i = pl.program_id(1)
    is_ctx = i < n_ctx_tiles

    u = jnp.where(is_ctx, ctx_ref[...], x_ref[...])
    row = jnp.where(is_ctx, ctx_row, b)
    shift = mod_ref[pl.ds(row, 1), 0:d]
    scale = mod_ref[pl.ds(row, 1), d:2 * d]
    ms = jnp.mean(u * u, axis=-1, keepdims=True)
    y = u * lax.rsqrt(ms + EPS) * ng_ref[...]
    hb = (y * (1.0 + scale) + shift).astype(BF16)

    def group(g, width=1024):
        return jnp.dot(hb, w_ref[:, g * 1024:g * 1024 + width], preferred_element_type=F32)

    def qk_epilogue(p, gain_ref, out_ref):
        p2 = (p * p).astype(BF16)
        for cch in range(d // 256):
            ss = jnp.dot(p2[:, cch * 256:(cch + 1) * 256], gsum_ref[...], preferred_element_type=F32)
            inv = lax.rsqrt(ss * (1.0 / MAP_W) + EPS)
            for hh in range(2):
                lo = cch * 256 + hh * HEAD_W
                un = p[:, lo:lo + HEAD_W] * inv[:, hh * HEAD_W:(hh + 1) * HEAD_W] * gain_ref[...]
                r = (un * cos_ref[...] + pltpu.roll(un, HEAD_W - 16, 1) * sa_ref[...]
                     + pltpu.roll(un, 16, 1) * sb_ref[...])
                out_ref[:, lo:lo + HEAD_W] = r.astype(out_ref.dtype)

    def lower_bound(ref):
        p = ref[...]
        e = jnp.exp(p - jnp.max(p, axis=0, keepdims=True))
        return e[0:1] / jnp.sum(e, axis=0, keepdims=True)

    def log_forget(a, lb):
        return jnp.log(lb + (1.0 - lb) * _sigmoid(a))

    qk_epilogue(group(G_ATTN_K), kg_ref, k_out)
    v_out[...] = group(G_ATTN_V).astype(BF16)
    hv_out[...] = group(G_HGRN_I).astype(BF16)
    gf_out[...] = log_forget(group(G_F_FWD), lower_bound(lbf_ref))
    gb_out[...] = log_forget(group(G_F_BWD), lower_bound(lbb_ref))

    @pl.when(jnp.logical_not(is_ctx))
    def _():
        qk_epilogue(group(G_ATTN_Q), qg_ref, q_out)
        a = group(G_HGRN_Q)
        hq_out[...] = (a * _sigmoid(a)).astype(BF16)
        a = group(G_ATTN_Z)
        za_out[...] = (a * _sigmoid(a)).astype(BF16)
        a = group(G_HGRN_Z)
        zh_out[...] = (a * _sigmoid(a)).astype(BF16)
        mg_out[:, 0:1024] = _sigmoid(group(G_MERGE)).astype(BF16)
        mg_out[:, 1024:2048] = _sigmoid(group(G_MERGE + 1)).astype(BF16)


def _proj_call(x, ctx, mod, norm_gain, w_bf, q_gain, k_gain, cos_t, sin_a, sin_b, lb_f, lb_b, gsum):
    bsz, t, d = x.shape
    lc = ctx.shape[1]
    tm = PROJ_ROWS
    assert lc % tm == 0 and t % tm == 0
    nct = lc // tm
    s = lc + t
    n_tiles = s // tm
    lat = lambda b, i: (b, jnp.maximum(i - nct, 0), 0)
    cat = lambda b, i: (b, i, 0)
    const2 = lambda b, i: (0, 0)
    kern = functools.partial(_proj_kernel, n_ctx_tiles=nct, ctx_row=bsz, d=d)
    bf = lambda n, w: jax.ShapeDtypeStruct((bsz, n, w), BF16)
    return pl.pallas_call(
        kern,
        grid=(bsz, n_tiles),
        in_specs=[
            pl.BlockSpec((None, tm, d), lat),
            pl.BlockSpec((None, tm, d), lambda b, i: (b, jnp.minimum(i, nct - 1), 0)),
            pl.BlockSpec(mod.shape, const2),
            pl.BlockSpec((1, d), const2),
            pl.BlockSpec(w_bf.shape, const2, pipeline_mode=pl.Buffered(1)),
            pl.BlockSpec((1, HEAD_W), const2),
            pl.BlockSpec((1, HEAD_W), const2),
            pl.BlockSpec((tm, HEAD_W), lambda b, i: (i, 0)),
            pl.BlockSpec((tm, HEAD_W), lambda b, i: (i, 0)),
            pl.BlockSpec((tm, HEAD_W), lambda b, i: (i, 0)),
            pl.BlockSpec(lb_f.shape, const2),
            pl.BlockSpec(lb_b.shape, const2),
            pl.BlockSpec(gsum.shape, const2),
        ],
        out_specs=[
            pl.BlockSpec((None, tm, d), cat),
            pl.BlockSpec((None, tm, d), cat),
            pl.BlockSpec((None, tm, d), cat),
            pl.BlockSpec((None, tm, d), cat),
            pl.BlockSpec((None, tm, d), cat),
            pl.BlockSpec((None, tm, d), lat),
            pl.BlockSpec((None, tm, d), lat),
            pl.BlockSpec((None, tm, d), lat),
            pl.BlockSpec((None, tm, d), lat),
            pl.BlockSpec((None, tm, 2 * d), lat),
        ],
        out_shape=[bf(s, d), bf(s, d), bf(s, d),
                   jax.ShapeDtypeStruct((bsz, s, d), F32), jax.ShapeDtypeStruct((bsz, s, d), F32),
                   bf(t, d), bf(t, d), bf(t, d), bf(t, d), bf(t, 2 * d)],
        compiler_params=pltpu.CompilerParams(
            dimension_semantics=("arbitrary", "arbitrary"), vmem_limit_bytes=VMEM_LIMIT),
        name="proj",
    )(x, ctx, mod, norm_gain, w_bf, q_gain, k_gain, cos_t, sin_a, sin_b, lb_f, lb_b, gsum)


def _attn_kernel(q_ref, k_ref, v_ref, lq1_ref, lk1_ref, lq2_ref, lk2_ref, sg_ref, o_ref):
    q = q_ref[...]
    k = k_ref[...]
    lane = lax.broadcasted_iota(jnp.int32, q.shape, 1)
    zero = jnp.zeros_like(q)
    q1 = jnp.where(lane < MAP_W, q, zero)
    q2 = jnp.where(lane >= MAP_W, q, zero)
    s1 = lax.dot_general(q1, k, _NT, preferred_element_type=F32)
    s2 = lax.dot_general(q2, k, _NT, preferred_element_type=F32)
    e1 = jnp.exp(s1 - jnp.max(s1, axis=-1, keepdims=True))
    e2 = jnp.exp(s2 - jnp.max(s2, axis=-1, keepdims=True))
    l1 = jnp.sum(e1, axis=-1, keepdims=True)
    l2 = jnp.sum(e2, axis=-1, keepdims=True)
    lam = (jnp.exp(jnp.sum(lq1_ref[...] * lk1_ref[...], axis=-1, keepdims=True))
           - jnp.exp(jnp.sum(lq2_ref[...] * lk2_ref[...], axis=-1, keepdims=True)) + LAM_INIT)
    w = e1 * (1.0 / l1) - e2 * (lam / l2)
    o = jnp.dot(w.astype(BF16), v_ref[...], preferred_element_type=F32)
    ms = jnp.mean(o * o, axis=-1, keepdims=True)
    o = o * lax.rsqrt(ms + EPS) * sg_ref[...] * (1.0 - LAM_INIT)
    o_ref[...] = o.astype(o_ref.dtype)


def _attn_call(q, k, v, lq1, lk1, lq2, lk2, subln_gain):
    bsz, t, d = q.shape
    s = k.shape[1]
    tq = ATTN_ROWS
    const = lambda b, h, i: (0, 0)
    return pl.pallas_call(
        _attn_kernel,
        grid=(bsz, HEADS, t // tq),
        in_specs=[
            pl.BlockSpec((None, tq, HEAD_W), lambda b, h, i: (b, i, h)),
            pl.BlockSpec((None, s, HEAD_W), lambda b, h, i: (b, 0, h)),
            pl.BlockSpec((None, s, HEAD_W), lambda b, h, i: (b, 0, h)),
            pl.BlockSpec((1, MAP_W), const), pl.BlockSpec((1, MAP_W), const),
            pl.BlockSpec((1, MAP_W), const), pl.BlockSpec((1, MAP_W), const),
            pl.BlockSpec((1, HEAD_W), const),
        ],
        out_specs=pl.BlockSpec((None, tq, HEAD_W), lambda b, h, i: (b, i, h)),
        out_shape=jax.ShapeDtypeStruct((bsz, t, d), BF16),
        compiler_params=pltpu.CompilerParams(
            dimension_semantics=("arbitrary", "arbitrary", "arbitrary"), vmem_limit_bytes=VMEM_LIMIT),
        name="attn",
    )(q, k, v, lq1, lk1, lq2, lk2, subln_gain)


def _hgrn_matrices(c, rev):
    t = np.arange(c)
    if not rev:
        incl = t[None, :] <= t[:, None]
        to_end = t[None, :] > t[:, None]
    else:
        incl = t[None, :] >= t[:, None]
        to_end = t[None, :] < t[:, None]
    blocks = [incl, to_end]
    for lvl in range(int(math.log2(c))):
        m = 1 << lvl
        e = np.zeros((c, c), bool)
        for tt in range(c):
            bs = (tt // (2 * m)) * (2 * m)
            if not rev:
                ref = bs + m - 1
                if tt > ref:
                    e[tt, ref + 1:tt + 1] = True
                else:
                    e[tt, tt + 1:ref + 1] = True
            else:
                ref = bs + m
                if tt < ref:
                    e[tt, tt:ref] = True
                else:
                    e[tt, ref:tt] = True
        blocks.append(e)
    ones = np.ones((8, c), bool)
    lat = np.concatenate(blocks + [ones], 0).astype(np.float32)
    ctx = np.concatenate([to_end, ones], 0).astype(np.float32)
    return jnp.asarray(lat, BF16), jnp.asarray(ctx, BF16)


def _split3(g):
    hi = g.astype(BF16)
    r1 = g - hi.astype(F32)
    mid = r1.astype(BF16)
    lo = (r1 - mid.astype(F32)).astype(BF16)
    return jnp.concatenate([hi, mid, lo], axis=1)


def _exponents(m_ref, g):
    w = g.shape[1]
    e = jnp.dot(m_ref[...], _split3(g), preferred_element_type=F32)
    return e[:, 0:w] + e[:, w:2 * w] + e[:, 2 * w:3 * w]


def _state_update(st, v, kk, e_to_end, e_total):
    kd = (kk * e_to_end).astype(BF16)
    vt = v.astype(F32).T.astype(BF16)
    return st * e_total + jnp.dot(vt, kd, preferred_element_type=F32)


def _ctx_chunk(st, g, v, m_ref):
    c = g.shape[0]
    ex = jnp.exp(_exponents(m_ref, g))
    kk = 1.0 - jnp.exp(g)
    return _state_update(st, v, kk, ex[0:c], ex[c:c + 1])


def _lat_chunk(st, g, q, v, m_ref, rev):
    c = g.shape[0]
    n_lvl = int(math.log2(c))
    ex = jnp.exp(_exponents(m_ref, g))
    kk = 1.0 - jnp.exp(g)
    qf = q.astype(F32)
    st_new = _state_update(st, v, kk, ex[c:2 * c], ex[(n_lvl + 2) * c:(n_lvl + 2) * c + 1])
    qd = (qf * ex[0:c]).astype(BF16)
    o = lax.dot_general(qd, st.astype(BF16), _NT, preferred_element_type=F32)

    row = lax.broadcasted_iota(jnp.int32, g.shape, 0)
    ti = lax.broadcasted_iota(jnp.int32, (c, c), 0)
    si = lax.broadcasted_iota(jnp.int32, (c, c), 1)
    a = jnp.where(ti == si, lax.dot_general(q, kk.astype(BF16), _NT, preferred_element_type=F32), 0.0)
    for lvl in range(n_lvl):
        m = 1 << lvl
        f = ex[(lvl + 2) * c:(lvl + 3) * c]
        upper = (row & m) != 0
        q_rows = jnp.logical_not(upper) if rev else upper
        ql = jnp.where(q_rows, qf * f, 0.0).astype(BF16)
        kl = jnp.where(q_rows, 0.0, kk * f).astype(BF16)
        p = lax.dot_general(ql, kl, _NT, preferred_element_type=F32)
        a = a + jnp.where((ti >> (lvl + 1)) == (si >> (lvl + 1)), p, 0.0)
    o = o + jnp.dot(a.astype(BF16), v, preferred_element_type=F32)
    return st_new, o


def _hgrn_kernel(q_ref, v_ref, gf_ref, gb_ref, mlf_ref, mcf_ref, mlb_ref, mcb_ref, hg_ref, o_ref, of_scr,
                 *, lc, c):
    t = q_ref.shape[0]
    dk = gf_ref.shape[1]
    dv = v_ref.shape[1]
    n_ctx = lc // c
    n_lat = t // c
    st0 = jnp.zeros((dv, dk), F32)

    def ctx_f(j, st):
        r = pl.multiple_of(j * c, c)
        return _ctx_chunk(st, gf_ref[pl.ds(r, c), :], v_ref[pl.ds(r, c), :], mcf_ref)

    st = lax.fori_loop(0, n_ctx, ctx_f, st0)

    def lat_f(j, st):
        r = pl.multiple_of(j * c, c)
        rs = pl.multiple_of(lc + j * c, c)
        st, o = _lat_chunk(st, gf_ref[pl.ds(rs, c), :], q_ref[pl.ds(r, c), :], v_ref[pl.ds(rs, c), :],
                           mlf_ref, False)
        of_scr[pl.ds(r, c), :] = o
        return st

    lax.fori_loop(0, n_lat, lat_f, st)

    def ctx_b(j, st):
        r = pl.multiple_of((n_ctx - 1 - j) * c, c)
        return _ctx_chunk(st, gb_ref[pl.ds(r, c), :], v_ref[pl.ds(r, c), :], mcb_ref)

    st = lax.fori_loop(0, n_ctx, ctx_b, st0)

    def lat_b(j, st):
        r = pl.multiple_of((n_lat - 1 - j) * c, c)
        rs = pl.multiple_of(lc + (n_lat - 1 - j) * c, c)
        st, o = _lat_chunk(st, gb_ref[pl.ds(rs, c), :], q_ref[pl.ds(r, c), :], v_ref[pl.ds(rs, c), :],
                           mlb_ref, True)
        o = o + of_scr[pl.ds(r, c), :]
        ms = jnp.mean(o * o, axis=-1, keepdims=True)
        o_ref[pl.ds(r, c), :] = (o * lax.rsqrt(ms + EPS) * hg_ref[...]).astype(o_ref.dtype)
        return st

    lax.fori_loop(0, n_lat, lat_b, st)


def _hgrn_call(hq, hv, gf, gb, hgrn_gain, lc):
    bsz, t, d = hq.shape
    s = hv.shape[1]
    c = CHUNK
    mlf, mcf = _hgrn_matrices(c, False)
    mlb, mcb = _hgrn_matrices(c, True)
    const = lambda b, h: (0, 0)
    kern = functools.partial(_hgrn_kernel, lc=lc, c=c)
    return pl.pallas_call(
        kern,
        grid=(bsz, HEADS),
        in_specs=[
            pl.BlockSpec((None, t, HEAD_W), lambda b, h: (b, 0, h)),
            pl.BlockSpec((None, s, HEAD_W), lambda b, h: (b, 0, h)),
            pl.BlockSpec((None, s, HEAD_W), lambda b, h: (b, 0, h)),
            pl.BlockSpec((None, s, HEAD_W), lambda b, h: (b, 0, h)),
            pl.BlockSpec(mlf.shape, const), pl.BlockSpec(mcf.shape, const),
            pl.BlockSpec(mlb.shape, const), pl.BlockSpec(mcb.shape, const),
            pl.BlockSpec((1, HEAD_W), const),
        ],
        out_specs=pl.BlockSpec((None, t, HEAD_W), lambda b, h: (b, 0, h)),
        out_shape=jax.ShapeDtypeStruct((bsz, t, d), BF16),
        scratch_shapes=[pltpu.VMEM((t, HEAD_W), F32)],
        compiler_params=pltpu.CompilerParams(
            dimension_semantics=("arbitrary", "arbitrary"), vmem_limit_bytes=VMEM_LIMIT),
        name="hgrn",
    )(hq, hv, gf, gb, mlf, mcf, mlb, mcb, hgrn_gain)


def _merge_kernel(oa_ref, oh_ref, za_ref, zh_ref, mg_ref, x_ref, mod_ref, wba_ref, wbh_ref, wo_ref, out_ref, *, d):
    b = pl.program_id(0)
    ya = oa_ref[...] * za_ref[...]
    yh = oh_ref[...] * zh_ref[...]
    ta = jnp.dot(ya, wba_ref[...], preferred_element_type=F32)
    th = jnp.dot(yh, wbh_ref[...], preferred_element_type=F32)
    mix = mg_ref[:, 0:d].astype(F32) * ta + mg_ref[:, d:2 * d].astype(F32) * th
    y = jnp.dot(mix.astype(BF16), wo_ref[...], preferred_element_type=F32)
    gate = mod_ref[pl.ds(b, 1), 2 * d:3 * d]
    out_ref[...] = x_ref[...] + gate * y


def _merge_call(oa, oh, za, zh, mg, x, mod, wba, wbh, wo):
    bsz, t, d = x.shape
    tm = MERGE_ROWS
    tok = lambda b, i: (b, i, 0)
    const = lambda b, i: (0, 0)
    return pl.pallas_call(
        functools.partial(_merge_kernel, d=d),
        grid=(bsz, t // tm),
        in_specs=[
            pl.BlockSpec((None, tm, d), tok), pl.BlockSpec((None, tm, d), tok),
            pl.BlockSpec((None, tm, d), tok), pl.BlockSpec((None, tm, d), tok),
            pl.BlockSpec((None, tm, 2 * d), tok), pl.BlockSpec((None, tm, d), tok),
            pl.BlockSpec(mod.shape, const),
            pl.BlockSpec((d, d), const), pl.BlockSpec((d, d), const), pl.BlockSpec((d, d), const),
        ],
        out_specs=pl.BlockSpec((None, tm, d), tok),
        out_shape=jax.ShapeDtypeStruct((bsz, t, d), F32),
        compiler_params=pltpu.CompilerParams(
            dimension_semantics=("arbitrary", "arbitrary"), vmem_limit_bytes=VMEM_LIMIT),
        name="merge",
    )(oa, oh, za, zh, mg, x, mod, wba, wbh, wo)


def _rope_tables(t, lc):
    rows = t // GRID_W
    row = jnp.repeat(jnp.arange(rows, dtype=F32), GRID_W)
    col = jnp.tile(jnp.arange(GRID_W, dtype=F32), rows)
    half = MAP_W // 2
    inv = ROPE_BASE ** (-jnp.arange(0, half, 2, dtype=F32) / half)
    ar = row[:, None] * inv[None, :]
    ac = col[:, None] * inv[None, :]
    ang = jnp.concatenate([ar, ar, ac, ac], axis=-1)
    ang = jnp.concatenate([ang, ang], axis=-1)
    cos, sin = jnp.cos(ang), jnp.sin(ang)
    first = (jnp.arange(HEAD_W) % (half)) < (half // 2)
    sin_a = jnp.where(first[None, :], -sin, 0.0)
    sin_b = jnp.where(first[None, :], 0.0, sin)
    pad = lambda a, v: jnp.concatenate([jnp.full((lc, HEAD_W), v, F32), a], axis=0)
    return pad(cos, 1.0), pad(sin_a, 0.0), pad(sin_b, 0.0)


def kernel(x, c, ctx, c_ctx, w_mod, b_mod, norm_gain, w_in, q_norm_gain, k_norm_gain, lambda_q1, lambda_k1,
           lambda_q2, lambda_k2, subln_gain, hgrn_lb_fwd, hgrn_lb_bwd, hgrn_norm_gain, w_br_attn, w_br_hgrn,
           w_out):
    bsz, t, d = x.shape
    lc = ctx.shape[1]
    assert w_mod.shape[0] == 1, "single-layer problem"

    mod_rows = ((bsz + 1 + 7) // 8) * 8
    cc = jnp.concatenate([c, c_ctx[None, :], jnp.zeros((mod_rows - bsz - 1, d), F32)], axis=0)
    mod = _mod_call(cc, w_mod[0], b_mod[0][None, :])

    cos_t, sin_a, sin_b = _rope_tables(t, lc)
    q_gain = jnp.tile(q_norm_gain[0], 2)[None, :] * (MAP_W ** -0.5)
    k_gain = jnp.tile(k_norm_gain[0], 2)[None, :]
    lane = np.arange(256)
    gsum = jnp.asarray((lane[:, None] // MAP_W) == (lane[None, :] // MAP_W), BF16)
    k_a, v_a, v_h, g_f, g_b, q_a, q_h, z_a, z_h, m_g = _proj_call(
        x, ctx, mod, norm_gain, w_in[0].astype(BF16), q_gain, k_gain, cos_t, sin_a, sin_b,
        hgrn_lb_fwd, hgrn_lb_bwd, gsum)

    o_a = _attn_call(q_a, k_a, v_a, lambda_q1, lambda_k1, lambda_q2, lambda_k2, subln_gain)
    o_h = _hgrn_call(q_h, v_h, g_f, g_b, hgrn_norm_gain, lc)
    return _merge_call(o_a, o_h, z_a, z_h, m_g, x, mod,
                       w_br_attn[0].astype(BF16), w_br_hgrn[0].astype(BF16), w_out[0].astype(BF16))
```

```python
import functools
import math

import jax
import jax.numpy as jnp
import numpy as np
from jax import lax
from jax.experimental import pallas as pl
from jax.experimental.pallas import tpu as pltpu

F32 = jnp.float32
BF16 = jnp.bfloat16

HEADS = 8
HEAD_W = 128
MAP_W = 64
GRID_W = 64
ROPE_BASE = 10000.0
EPS = 1e-6
LOG2E = 1.4426950408889634
MAX_SOFTMAX_SHIFT = 50.0
LAM_INIT = 0.8 - 0.6 * math.exp(-0.3 * 0)

PROJ_ROWS = 256
ATTN_ROWS = 256
MERGE_ROWS = 512
CHUNK = 256
VMEM_LIMIT = 56 * 1024 * 1024

G_ATTN_K, G_ATTN_V, G_HGRN_I, G_F_FWD, G_F_BWD, G_ATTN_Q, G_HGRN_Q, G_ATTN_Z, G_HGRN_Z, G_MERGE = range(10)

_NT = (((1,), (1,)), ((), ()))


def _sigmoid(a):
    return 1.0 / (1.0 + jnp.exp(-a))


def _mod_kernel(c_ref, w_ref, b_ref, o_ref):
    a = c_ref[...]
    s = a * _sigmoid(a)
    o_ref[...] = jnp.dot(s, w_ref[...], precision=lax.Precision.HIGHEST,
                         preferred_element_type=F32) + b_ref[...]


def _mod_call(cc, w_mod, b_mod):
    rows, d = cc.shape
    n = w_mod.shape[1]
    bn = 1024
    return pl.pallas_call(
        _mod_kernel,
        grid=(n // bn,),
        in_specs=[pl.BlockSpec((rows, d), lambda j: (0, 0)),
                  pl.BlockSpec((d, bn), lambda j: (0, j)),
                  pl.BlockSpec((1, bn), lambda j: (0, j))],
        out_specs=pl.BlockSpec((rows, bn), lambda j: (0, j)),
        out_shape=jax.ShapeDtypeStruct((rows, n), F32),
        name="mod",
    )(cc, w_mod, b_mod)


def _proj_kernel(x_ref, ctx_ref, mod_ref, ng_ref, w_ref, qg_ref, kg_ref, cos_ref, sa_ref, sb_ref,
                 lbf_ref, lbb_ref, gsum_ref,
                 k_out, v_out, hv_out, gf_out, gb_out, q_out, hq_out, za_out, zh_out, mg_out,
                 *, n_ctx_tiles, ctx_row, d):
    b = pl.program_id(0)
    i = pl.program_id(1)
    is_ctx = i < n_ctx_tiles

    u = jnp.where(is_ctx, ctx_ref[...], x_ref[...])
    row = jnp.where(is_ctx, ctx_row, b)
    shift = mod_ref[pl.ds(row, 1), 0:d]
    scale = mod_ref[pl.ds(row, 1), d:2 * d]
    ms = jnp.mean(u * u, axis=-1, keepdims=True)
    y = u * lax.rsqrt(ms + EPS) * ng_ref[...]
    hb = (y * (1.0 + scale) + shift).astype(BF16)

    def group(g, width=1024):
        return jnp.dot(hb, w_ref[:, g * 1024:g * 1024 + width], preferred_element_type=F32)

    def qk_epilogue(p, gain_ref, out_ref):
        p2 = (p * p).astype(BF16)
        for cch in range(d // 256):
            ss = jnp.dot(p2[:, cch * 256:(cch + 1) * 256], gsum_ref[...], preferred_element_type=F32)
            inv = lax.rsqrt(ss * (1.0 / MAP_W) + EPS)
            for hh in range(2):
                lo = cch * 256 + hh * HEAD_W
                un = p[:, lo:lo + HEAD_W] * inv[:, hh * HEAD_W:(hh + 1) * HEAD_W] * gain_ref[...]
                r = (un * cos_ref[...] + pltpu.roll(un, HEAD_W - 16, 1) * sa_ref[...]
                     + pltpu.roll(un, 16, 1) * sb_ref[...])
                out_ref[:, lo:lo + HEAD_W] = r.astype(out_ref.dtype)

    def lower_bound(ref):
        p = ref[...]
        e = jnp.exp(p - jnp.max(p, axis=0, keepdims=True))
        return e[0:1] / jnp.sum(e, axis=0, keepdims=True)

    def log_forget(a, lb):
        return jnp.log(lb + (1.0 - lb) * _sigmoid(a)) * LOG2E

    qk_epilogue(group(G_ATTN_K), kg_ref, k_out)
    v_out[...] = group(G_ATTN_V).astype(BF16)
    hv_out[...] = group(G_HGRN_I).astype(BF16)
    gf_out[...] = log_forget(group(G_F_FWD), lower_bound(lbf_ref))
    gb_out[...] = log_forget(group(G_F_BWD), lower_bound(lbb_ref))

    @pl.when(jnp.logical_not(is_ctx))
    def _():
        qk_epilogue(group(G_ATTN_Q), qg_ref, q_out)
        a = group(G_HGRN_Q)
        hq_out[...] = (a * _sigmoid(a)).astype(BF16)
        a = group(G_ATTN_Z)
        za_out[...] = (a * _sigmoid(a)).astype(BF16)
        a = group(G_HGRN_Z)
        zh_out[...] = (a * _sigmoid(a)).astype(BF16)
        mg_out[:, 0:1024] = _sigmoid(group(G_MERGE)).astype(BF16)
        mg_out[:, 1024:2048] = _sigmoid(group(G_MERGE + 1)).astype(BF16)


def _proj_call(x, ctx, mod, norm_gain, w_bf, q_gain, k_gain, cos_t, sin_a, sin_b, lb_f, lb_b, gsum):
    bsz, t, d = x.shape
    lc = ctx.shape[1]
    tm = PROJ_ROWS
    assert lc % tm == 0 and t % tm == 0
    nct = lc // tm
    s = lc + t
    n_tiles = s // tm
    lat = lambda b, i: (b, jnp.maximum(i - nct, 0), 0)
    cat = lambda b, i: (b, i, 0)
    const2 = lambda b, i: (0, 0)
    kern = functools.partial(_proj_kernel, n_ctx_tiles=nct, ctx_row=bsz, d=d)
    bf = lambda n, w: jax.ShapeDtypeStruct((bsz, n, w), BF16)
    return pl.pallas_call(
        kern,
        grid=(bsz, n_tiles),
        in_specs=[
            pl.BlockSpec((None, tm, d), lat),
            pl.BlockSpec((None, tm, d), lambda b, i: (b, jnp.minimum(i, nct - 1), 0)),
            pl.BlockSpec(mod.shape, const2),
            pl.BlockSpec((1, d), const2),
            pl.BlockSpec(w_bf.shape, const2, pipeline_mode=pl.Buffered(1)),
            pl.BlockSpec((1, HEAD_W), const2),
            pl.BlockSpec((1, HEAD_W), const2),
            pl.BlockSpec((tm, HEAD_W), lambda b, i: (i, 0)),
            pl.BlockSpec((tm, HEAD_W), lambda b, i: (i, 0)),
            pl.BlockSpec((tm, HEAD_W), lambda b, i: (i, 0)),
            pl.BlockSpec(lb_f.shape, const2),
            pl.BlockSpec(lb_b.shape, const2),
            pl.BlockSpec(gsum.shape, const2),
        ],
        out_specs=[
            pl.BlockSpec((None, tm, d), cat),
            pl.BlockSpec((None, tm, d), cat),
            pl.BlockSpec((None, tm, d), cat),
            pl.BlockSpec((None, tm, d), cat),
            pl.BlockSpec((None, tm, d), cat),
            pl.BlockSpec((None, tm, d), lat),
            pl.BlockSpec((None, tm, d), lat),
            pl.BlockSpec((None, tm, d), lat),
            pl.BlockSpec((None, tm, d), lat),
            pl.BlockSpec((None, tm, 2 * d), lat),
        ],
        out_shape=[bf(s, d), bf(s, d), bf(s, d),
                   jax.ShapeDtypeStruct((bsz, s, d), F32), jax.ShapeDtypeStruct((bsz, s, d), F32),
                   bf(t, d), bf(t, d), bf(t, d), bf(t, d), bf(t, 2 * d)],
        compiler_params=pltpu.CompilerParams(
            dimension_semantics=("arbitrary", "arbitrary"), vmem_limit_bytes=VMEM_LIMIT),
        name="proj",
    )(x, ctx, mod, norm_gain, w_bf, q_gain, k_gain, cos_t, sin_a, sin_b, lb_f, lb_b, gsum)


def _attn_kernel(bounded_ref, shift_ref, q_ref, k_ref, v_ref, lq1_ref, lk1_ref, lq2_ref, lk2_ref, sg_ref, o_ref):
    q = q_ref[...]
    k = k_ref[...]
    lane = lax.broadcasted_iota(jnp.int32, q.shape, 1)
    zero = jnp.zeros_like(q)
    q1 = jnp.where(lane < MAP_W, q, zero)
    q2 = jnp.where(lane >= MAP_W, q, zero)
    lam = (jnp.exp(jnp.sum(lq1_ref[...] * lk1_ref[...], axis=-1, keepdims=True))
           - jnp.exp(jnp.sum(lq2_ref[...] * lk2_ref[...], axis=-1, keepdims=True)) + LAM_INIT)

    def finish(w):
        o = jnp.dot(w, v_ref[...], preferred_element_type=F32)
        ms = jnp.mean(o * o, axis=-1, keepdims=True)
        o = o * lax.rsqrt(ms + EPS) * sg_ref[...] * (1.0 - LAM_INIT)
        o_ref[...] = o.astype(o_ref.dtype)

    @pl.when(bounded_ref[0] != 0)
    def _():
        shift = shift_ref[0]
        e1 = jnp.exp2(lax.dot_general(q1, k, _NT, preferred_element_type=F32) - shift)
        e2 = jnp.exp2(lax.dot_general(q2, k, _NT, preferred_element_type=F32) - shift)
        c1 = 1.0 / jnp.sum(e1, axis=-1, keepdims=True)
        c2 = lam / jnp.sum(e2, axis=-1, keepdims=True)
        finish(e1.astype(BF16) * c1.astype(BF16) - e2.astype(BF16) * c2.astype(BF16))

    @pl.when(bounded_ref[0] == 0)
    def _():
        s1 = lax.dot_general(q1, k, _NT, preferred_element_type=F32)
        s2 = lax.dot_general(q2, k, _NT, preferred_element_type=F32)
        e1 = jnp.exp2(s1 - jnp.max(s1, axis=-1, keepdims=True))
        e2 = jnp.exp2(s2 - jnp.max(s2, axis=-1, keepdims=True))
        c1 = 1.0 / jnp.sum(e1, axis=-1, keepdims=True)
        c2 = lam / jnp.sum(e2, axis=-1, keepdims=True)
        finish((e1 * c1 - e2 * c2).astype(BF16))


def _attn_call(bounded, shift, q, k, v, lq1, lk1, lq2, lk2, subln_gain):
    bsz, t, d = q.shape
    s = k.shape[1]
    tq = ATTN_ROWS
    const = lambda b, h, i: (0, 0)
    return pl.pallas_call(
        _attn_kernel,
        grid=(bsz, HEADS, t // tq),
        in_specs=[
            pl.BlockSpec(memory_space=pltpu.SMEM),
            pl.BlockSpec(memory_space=pltpu.SMEM),
            pl.BlockSpec((None, tq, HEAD_W), lambda b, h, i: (b, i, h)),
            pl.BlockSpec((None, s, HEAD_W), lambda b, h, i: (b, 0, h)),
            pl.BlockSpec((None, s, HEAD_W), lambda b, h, i: (b, 0, h)),
            pl.BlockSpec((1, MAP_W), const), pl.BlockSpec((1, MAP_W), const),
            pl.BlockSpec((1, MAP_W), const), pl.BlockSpec((1, MAP_W), const),
            pl.BlockSpec((1, HEAD_W), const),
        ],
        out_specs=pl.BlockSpec((None, tq, HEAD_W), lambda b, h, i: (b, i, h)),
        out_shape=jax.ShapeDtypeStruct((bsz, t, d), BF16),
        compiler_params=pltpu.CompilerParams(
            dimension_semantics=("arbitrary", "arbitrary", "arbitrary"), vmem_limit_bytes=VMEM_LIMIT),
        name="attn",
    )(bounded, shift, q, k, v, lq1, lk1, lq2, lk2, subln_gain)


def _prefix_matrix(c, rev):
    t = np.arange(c)
    m = (t[None, :] >= t[:, None]) if rev else (t[None, :] <= t[:, None])
    return jnp.asarray(m.astype(np.float32), BF16)


def _split2(g):
    hi = g.astype(BF16)
    mid = (g - hi.astype(F32)).astype(BF16)
    return jnp.concatenate([hi, mid], axis=1)


def _cum_decay(l_ref, g, b_scr, rev):
    c, w = g.shape
    e = jnp.dot(l_ref[...], _split2(g), preferred_element_type=F32)
    b = e[:, 0:w] + e[:, w:2 * w]
    b_scr[...] = b
    tot = b_scr[0:1, :] if rev else b_scr[c - 1:c, :]
    return b, tot


def _state_update(st, v, kk, e_to_end, e_total):
    kd = (kk * e_to_end).astype(BF16)
    vt = v.astype(F32).T.astype(BF16)
    return st * e_total + jnp.dot(vt, kd, preferred_element_type=F32)


def _ctx_chunk(st, g, v, l_ref, b_scr, rev):
    b, tot = _cum_decay(l_ref, g, b_scr, rev)
    kk = 1.0 - jnp.exp2(g)
    return _state_update(st, v, kk, jnp.exp2(tot - b), jnp.exp2(tot))


def _level_operand(m, qf, kk, eg, b, b_scr, rev):
    c, w = b.shape
    ref = m if rev else m - 1
    bcast = lambda r, n: jnp.broadcast_to(b_scr[r:r + 1, :], (n, w))
    if m >= 8:
        pieces = []
        for j in range(c // (2 * m)):
            lo = slice(j * 2 * m, j * 2 * m + m)
            hi = slice(j * 2 * m + m, (j + 1) * 2 * m)
            r = bcast(j * 2 * m + ref, m)
            if rev:
                pieces += [qf[lo] * jnp.exp2(b[lo] - r), kk[hi] * jnp.exp2(r - b[hi])]
            else:
                pieces += [kk[lo] * jnp.exp2(r - b[lo]), qf[hi] * jnp.exp2(b[hi] - r)]
        return jnp.concatenate(pieces, axis=0).astype(BF16)
    row = lax.broadcasted_iota(jnp.int32, (c, w), 0)
    upper = (row & m) != 0
    q_rows = jnp.logical_not(upper) if rev else upper
    if m == 1:
        f = jnp.where(q_rows, eg, 1.0)
    else:
        if m == 4:
            r = jnp.concatenate([bcast(8 * j + ref, 8) for j in range(c // 8)], axis=0)
        else:
            sub = lax.broadcasted_iota(jnp.int32, (8, w), 0)
            r = jnp.concatenate([jnp.where(sub < 4, bcast(8 * j + ref, 8), bcast(8 * j + 4 + ref, 8))
                                 for j in range(c // 8)], axis=0)
        f = jnp.exp2(-jnp.abs(b - r))
    return (jnp.where(q_rows, qf, kk) * f).astype(BF16)


def _lat_chunk(st, g, q, v, l_ref, b_scr, rev):
    c, w = g.shape
    h = c // 2
    n_lvl = int(math.log2(c))
    b, tot = _cum_decay(l_ref, g, b_scr, rev)
    eg = jnp.exp2(g)
    kk = 1.0 - eg
    qf = q.astype(F32)
    st_new = _state_update(st, v, kk, jnp.exp2(tot - b), jnp.exp2(tot))
    qd = (qf * jnp.exp2(b)).astype(BF16)
    o = lax.dot_general(qd, st.astype(BF16), _NT, preferred_element_type=F32)
    o = o + jnp.sum(qf * kk, axis=-1, keepdims=True) * v.astype(F32)

    x = _level_operand(h, qf, kk, eg, b, b_scr, rev)
    p = lax.dot_general(x, x, _NT, preferred_element_type=F32)
    a_off = p[0:h, h:c] if rev else p[h:c, 0:h]
    ti = lax.broadcasted_iota(jnp.int32, (h, h), 0)
    si = lax.broadcasted_iota(jnp.int32, (h, h), 1)
    differ = ti ^ si
    a0 = a1 = None
    for lvl in reversed(range(n_lvl - 1)):
        m = 1 << lvl
        x = _level_operand(m, qf, kk, eg, b, b_scr, rev)
        p = lax.dot_general(x, x, _NT, preferred_element_type=F32)
        if a0 is None:
            a0, a1 = p[0:h, 0:h], p[h:c, h:c]
        else:
            same_block = differ < 2 * m
            a0 = jnp.where(same_block, p[0:h, 0:h], a0)
            a1 = jnp.where(same_block, p[h:c, h:c], a1)
    valid = (ti < si) if rev else (ti > si)
    a0 = jnp.where(valid, a0, 0.0).astype(BF16)
    a1 = jnp.where(valid, a1, 0.0).astype(BF16)
    a_off = a_off.astype(BF16)
    if rev:
        o_lo = jnp.dot(jnp.concatenate([a0, a_off], axis=1), v, preferred_element_type=F32)
        o_hi = jnp.dot(a1, v[h:c], preferred_element_type=F32)
    else:
        o_lo = jnp.dot(a0, v[0:h], preferred_element_type=F32)
        o_hi = jnp.dot(jnp.concatenate([a_off, a1], axis=1), v, preferred_element_type=F32)
    return st_new, o + jnp.concatenate([o_lo, o_hi], axis=0)


def _hgrn_kernel(q_ref, v_ref, gf_ref, gb_ref, lf_ref, lb_ref, hg_ref, o_ref, of_scr, ob_scr, bf_scr, bb_scr,
                 *, lc, c):
    t = q_ref.shape[0]
    dk = gf_ref.shape[1]
    dv = v_ref.shape[1]
    n_ctx = lc // c
    n_lat = t // c
    st0 = jnp.zeros((dv, dk), F32)

    def ctx(j, carry):
        st_f, st_b = carry
        rf = pl.multiple_of(j * c, c)
        rb = pl.multiple_of((n_ctx - 1 - j) * c, c)
        st_f = _ctx_chunk(st_f, gf_ref[pl.ds(rf, c), :], v_ref[pl.ds(rf, c), :], lf_ref, bf_scr, False)
        st_b = _ctx_chunk(st_b, gb_ref[pl.ds(rb, c), :], v_ref[pl.ds(rb, c), :], lb_ref, bb_scr, True)
        return st_f, st_b

    carry = lax.fori_loop(0, n_ctx, ctx, (st0, st0))

    def lat(j, carry):
        st_f, st_b = carry
        rf = pl.multiple_of(j * c, c)
        rb = pl.multiple_of((n_lat - 1 - j) * c, c)
        sf = pl.multiple_of(lc + j * c, c)
        sb = pl.multiple_of(lc + (n_lat - 1 - j) * c, c)
        st_f, o_f = _lat_chunk(st_f, gf_ref[pl.ds(sf, c), :], q_ref[pl.ds(rf, c), :], v_ref[pl.ds(sf, c), :],
                               lf_ref, bf_scr, False)
        of_scr[pl.ds(rf, c), :] = o_f
        st_b, o_b = _lat_chunk(st_b, gb_ref[pl.ds(sb, c), :], q_ref[pl.ds(rb, c), :], v_ref[pl.ds(sb, c), :],
                               lb_ref, bb_scr, True)
        ob_scr[pl.ds(rb, c), :] = o_b
        return st_f, st_b

    lax.fori_loop(0, n_lat, lat, carry)

    def combine(j, _):
        r = pl.multiple_of(j * c, c)
        o = of_scr[pl.ds(r, c), :] + ob_scr[pl.ds(r, c), :]
        ms = jnp.mean(o * o, axis=-1, keepdims=True)
        o_ref[pl.ds(r, c), :] = (o * lax.rsqrt(ms + EPS) * hg_ref[...]).astype(o_ref.dtype)
        return 0

    lax.fori_loop(0, n_lat, combine, 0)


def _hgrn_call(hq, hv, gf, gb, hgrn_gain, lc):
    bsz, t, d = hq.shape
    s = hv.shape[1]
    c = CHUNK
    assert lc % c == 0 and t % c == 0
    lf = _prefix_matrix(c, False)
    lb = _prefix_matrix(c, True)
    const = lambda b, h: (0, 0)
    kern = functools.partial(_hgrn_kernel, lc=lc, c=c)
    return pl.pallas_call(
        kern,
        grid=(bsz, HEADS),
        in_specs=[
            pl.BlockSpec((None, t, HEAD_W), lambda b, h: (b, 0, h)),
            pl.BlockSpec((None, s, HEAD_W), lambda b, h: (b, 0, h)),
            pl.BlockSpec((None, s, HEAD_W), lambda b, h: (b, 0, h)),
            pl.BlockSpec((None, s, HEAD_W), lambda b, h: (b, 0, h)),
            pl.BlockSpec(lf.shape, const), pl.BlockSpec(lb.shape, const),
            pl.BlockSpec((1, HEAD_W), const),
        ],
        out_specs=pl.BlockSpec((None, t, HEAD_W), lambda b, h: (b, 0, h)),
        out_shape=jax.ShapeDtypeStruct((bsz, t, d), BF16),
        scratch_shapes=[pltpu.VMEM((t, HEAD_W), F32), pltpu.VMEM((t, HEAD_W), F32),
                        pltpu.VMEM((c, HEAD_W), F32), pltpu.VMEM((c, HEAD_W), F32)],
        compiler_params=pltpu.CompilerParams(
            dimension_semantics=("arbitrary", "arbitrary"), vmem_limit_bytes=VMEM_LIMIT),
        name="hgrn",
    )(hq, hv, gf, gb, lf, lb, hgrn_gain)


def _merge_kernel(oa_ref, oh_ref, za_ref, zh_ref, mg_ref, x_ref, mod_ref, wba_ref, wbh_ref, wo_ref, out_ref, *, d):
    b = pl.program_id(0)
    ya = oa_ref[...] * za_ref[...]
    yh = oh_ref[...] * zh_ref[...]
    ta = jnp.dot(ya, wba_ref[...], preferred_element_type=F32)
    th = jnp.dot(yh, wbh_ref[...], preferred_element_type=F32)
    mix = mg_ref[:, 0:d].astype(F32) * ta + mg_ref[:, d:2 * d].astype(F32) * th
    y = jnp.dot(mix.astype(BF16), wo_ref[...], preferred_element_type=F32)
    gate = mod_ref[pl.ds(b, 1), 2 * d:3 * d]
    out_ref[...] = x_ref[...] + gate * y


def _merge_call(oa, oh, za, zh, mg, x, mod, wba, wbh, wo):
    bsz, t, d = x.shape
    tm = MERGE_ROWS
    tok = lambda b, i: (b, i, 0)
    const = lambda b, i: (0, 0)
    return pl.pallas_call(
        functools.partial(_merge_kernel, d=d),
        grid=(bsz, t // tm),
        in_specs=[
            pl.BlockSpec((None, tm, d), tok), pl.BlockSpec((None, tm, d), tok),
            pl.BlockSpec((None, tm, d), tok), pl.BlockSpec((None, tm, d), tok),
            pl.BlockSpec((None, tm, 2 * d), tok), pl.BlockSpec((None, tm, d), tok),
            pl.BlockSpec(mod.shape, const),
            pl.BlockSpec((d, d), const), pl.BlockSpec((d, d), const), pl.BlockSpec((d, d), const),
        ],
        out_specs=pl.BlockSpec((None, tm, d), tok),
        out_shape=jax.ShapeDtypeStruct((bsz, t, d), F32),
        compiler_params=pltpu.CompilerParams(
            dimension_semantics=("arbitrary", "arbitrary"), vmem_limit_bytes=VMEM_LIMIT),
        name="merge",
    )(oa, oh, za, zh, mg, x, mod, wba, wbh, wo)


def _rope_tables(t, lc):
    rows = t // GRID_W
    row = jnp.repeat(jnp.arange(rows, dtype=F32), GRID_W)
    col = jnp.tile(jnp.arange(GRID_W, dtype=F32), rows)
    half = MAP_W // 2
    inv = ROPE_BASE ** (-jnp.arange(0, half, 2, dtype=F32) / half)
    ar = row[:, None] * inv[None, :]
    ac = col[:, None] * inv[None, :]
    ang = jnp.concatenate([ar, ar, ac, ac], axis=-1)
    ang = jnp.concatenate([ang, ang], axis=-1)
    cos, sin = jnp.cos(ang), jnp.sin(ang)
    first = (jnp.arange(HEAD_W) % (half)) < (half // 2)
    sin_a = jnp.where(first[None, :], -sin, 0.0)
    sin_b = jnp.where(first[None, :], 0.0, sin)
    pad = lambda a, v: jnp.concatenate([jnp.full((lc, HEAD_W), v, F32), a], axis=0)
    return pad(cos, 1.0), pad(sin_a, 0.0), pad(sin_b, 0.0)


def kernel(x, c, ctx, c_ctx, w_mod, b_mod, norm_gain, w_in, q_norm_gain, k_norm_gain, lambda_q1, lambda_k1,
           lambda_q2, lambda_k2, subln_gain, hgrn_lb_fwd, hgrn_lb_bwd, hgrn_norm_gain, w_br_attn, w_br_hgrn,
           w_out):
    bsz, t, d = x.shape
    lc = ctx.shape[1]
    assert w_mod.shape[0] == 1, "single-layer problem"

    mod_rows = ((bsz + 1 + 7) // 8) * 8
    cc = jnp.concatenate([c, c_ctx[None, :], jnp.zeros((mod_rows - bsz - 1, d), F32)], axis=0)
    mod = _mod_call(cc, w_mod[0], b_mod[0][None, :])

    cos_t, sin_a, sin_b = _rope_tables(t, lc)
    q_gain = jnp.tile(q_norm_gain[0], 2)[None, :] * (MAP_W ** -0.5 * LOG2E)
    k_gain = jnp.tile(k_norm_gain[0], 2)[None, :]
    shift = (1.03 * MAP_W) * jnp.max(jnp.abs(q_gain)) * jnp.max(jnp.abs(k_gain)) + 0.1
    bounded = (shift <= MAX_SOFTMAX_SHIFT).astype(jnp.int32).reshape(1)
    shift = shift.astype(F32).reshape(1)
    lane = np.arange(256)
    gsum = jnp.asarray((lane[:, None] // MAP_W) == (lane[None, :] // MAP_W), BF16)
    k_a, v_a, v_h, g_f, g_b, q_a, q_h, z_a, z_h, m_g = _proj_call(
        x, ctx, mod, norm_gain, w_in[0].astype(BF16), q_gain, k_gain, cos_t, sin_a, sin_b,
        hgrn_lb_fwd, hgrn_lb_bwd, gsum)

    o_a = _attn_call(bounded, shift, q_a, k_a, v_a, lambda_q1, lambda_k1, lambda_q2, lambda_k2, subln_gain)
    o_h = _hgrn_call(q_h, v_h, g_f, g_b, hgrn_norm_gain, lc)
    return _merge_call(o_a, o_h, z_a, z_h, m_g, x, mod,
                       w_br_attn[0].astype(BF16), w_br_hgrn[0].astype(BF16), w_out[0].astype(BF16))
```

```python
import functools
import math

import jax
import jax.numpy as jnp
import numpy as np
from jax import lax
from jax.experimental import pallas as pl
from jax.experimental.pallas import tpu as pltpu

F32 = jnp.float32
BF16 = jnp.bfloat16

HEADS = 8
HEAD_W = 128
MAP_W = 64
GRID_W = 64
ROPE_BASE = 10000.0
EPS = 1e-6
LOG2E = 1.4426950408889634
MAX_SOFTMAX_SHIFT = 50.0
LAM_INIT = 0.8 - 0.6 * math.exp(-0.3 * 0)

PROJ_ROWS = 256
ATTN_ROWS = 256
KEY_CHUNK = 1024
MERGE_ROWS = 512
CHUNK = 256
LAT_UNROLL = 2
VMEM_LIMIT = 56 * 1024 * 1024

G_ATTN_K, G_ATTN_V, G_HGRN_I, G_F_FWD, G_F_BWD, G_ATTN_Q, G_HGRN_Q, G_ATTN_Z, G_HGRN_Z, G_MERGE = range(10)

_NT = (((1,), (1,)), ((), ()))


def _sigmoid(a):
    return 1.0 / (1.0 + jnp.exp(-a))


def _mod_kernel(c_ref, w_ref, b_ref, o_ref):
    a = c_ref[...]
    s = a * _sigmoid(a)
    o_ref[...] = jnp.dot(s, w_ref[...], precision=lax.Precision.HIGHEST,
                         preferred_element_type=F32) + b_ref[...]


def _mod_call(cc, w_mod, b_mod):
    rows, d = cc.shape
    n = w_mod.shape[1]
    bn = 1024
    return pl.pallas_call(
        _mod_kernel,
        grid=(n // bn,),
        in_specs=[pl.BlockSpec((rows, d), lambda j: (0, 0)),
                  pl.BlockSpec((d, bn), lambda j: (0, j)),
                  pl.BlockSpec((1, bn), lambda j: (0, j))],
        out_specs=pl.BlockSpec((rows, bn), lambda j: (0, j)),
        out_shape=jax.ShapeDtypeStruct((rows, n), F32),
        name="mod",
    )(cc, w_mod, b_mod)


def _proj_kernel(x_ref, ctx_ref, mod_ref, ng_ref, w_ref, qg_ref, kg_ref, cos_ref, sa_ref, sb_ref,
                 lbf_ref, lbb_ref, gsum_ref,
                 k_out, v_out, hv_out, gf_out, gb_out, q_out, hq_out, za_out, zh_out, mg_out,
                 *, n_ctx_tiles, ctx_row, d):
    b = pl.program_id(0)
    i = pl.program_id(1)
    is_ctx = i < n_ctx_tiles

    u = jnp.where(is_ctx, ctx_ref[...], x_ref[...])
    row = jnp.where(is_ctx, ctx_row, b)
    shift = mod_ref[pl.ds(row, 1), 0:d]
    scale = mod_ref[pl.ds(row, 1), d:2 * d]
    ms = jnp.mean(u * u, axis=-1, keepdims=True)
    y = u * lax.rsqrt(ms + EPS) * ng_ref[...]
    hb = (y * (1.0 + scale) + shift).astype(BF16)

    def group(g, width=1024):
        return jnp.dot(hb, w_ref[:, g * 1024:g * 1024 + width], preferred_element_type=F32)

    def qk_epilogue(p, gain_ref, out_ref):
        p2 = (p * p).astype(BF16)
        for cch in range(d // 256):
            ss = jnp.dot(p2[:, cch * 256:(cch + 1) * 256], gsum_ref[...], preferred_element_type=F32)
            inv = lax.rsqrt(ss * (1.0 / MAP_W) + EPS)
            for hh in range(2):
                lo = cch * 256 + hh * HEAD_W
                un = p[:, lo:lo + HEAD_W] * inv[:, hh * HEAD_W:(hh + 1) * HEAD_W] * gain_ref[...]
                r = (un * cos_ref[...] + pltpu.roll(un, HEAD_W - 16, 1) * sa_ref[...]
                     + pltpu.roll(un, 16, 1) * sb_ref[...])
                out_ref[:, lo:lo + HEAD_W] = r.astype(out_ref.dtype)

    def lower_bound(ref):
        p = ref[...]
        e = jnp.exp(p - jnp.max(p, axis=0, keepdims=True))
        return e[0:1] / jnp.sum(e, axis=0, keepdims=True)

    def log_forget(a, lb):
        return jnp.log(lb + (1.0 - lb) * _sigmoid(a)) * LOG2E

    qk_epilogue(group(G_ATTN_K), kg_ref, k_out)
    v_out[...] = group(G_ATTN_V).astype(BF16)
    hv_out[...] = group(G_HGRN_I).astype(BF16)
    gf_out[...] = log_forget(group(G_F_FWD), lower_bound(lbf_ref))
    gb_out[...] = log_forget(group(G_F_BWD), lower_bound(lbb_ref))

    @pl.when(jnp.logical_not(is_ctx))
    def _():
        qk_epilogue(group(G_ATTN_Q), qg_ref, q_out)
        a = group(G_HGRN_Q)
        hq_out[...] = (a * _sigmoid(a)).astype(BF16)
        a = group(G_ATTN_Z)
        za_out[...] = (a * _sigmoid(a)).astype(BF16)
        a = group(G_HGRN_Z)
        zh_out[...] = (a * _sigmoid(a)).astype(BF16)
        mg_out[:, 0:1024] = _sigmoid(group(G_MERGE)).astype(BF16)
        mg_out[:, 1024:2048] = _sigmoid(group(G_MERGE + 1)).astype(BF16)


def _proj_call(x, ctx, mod, norm_gain, w_bf, q_gain, k_gain, cos_t, sin_a, sin_b, lb_f, lb_b, gsum):
    bsz, t, d = x.shape
    lc = ctx.shape[1]
    tm = PROJ_ROWS
    assert lc % tm == 0 and t % tm == 0
    nct = lc // tm
    s = lc + t
    n_tiles = s // tm
    lat = lambda b, i: (b, jnp.maximum(i - nct, 0), 0)
    cat = lambda b, i: (b, i, 0)
    const2 = lambda b, i: (0, 0)
    kern = functools.partial(_proj_kernel, n_ctx_tiles=nct, ctx_row=bsz, d=d)
    bf = lambda n, w: jax.ShapeDtypeStruct((bsz, n, w), BF16)
    return pl.pallas_call(
        kern,
        grid=(bsz, n_tiles),
        in_specs=[
            pl.BlockSpec((None, tm, d), lat),
            pl.BlockSpec((None, tm, d), lambda b, i: (b, jnp.minimum(i, nct - 1), 0)),
            pl.BlockSpec(mod.shape, const2),
            pl.BlockSpec((1, d), const2),
            pl.BlockSpec(w_bf.shape, const2, pipeline_mode=pl.Buffered(1)),
            pl.BlockSpec((1, HEAD_W), const2),
            pl.BlockSpec((1, HEAD_W), const2),
            pl.BlockSpec((tm, HEAD_W), lambda b, i: (i, 0)),
            pl.BlockSpec((tm, HEAD_W), lambda b, i: (i, 0)),
            pl.BlockSpec((tm, HEAD_W), lambda b, i: (i, 0)),
            pl.BlockSpec(lb_f.shape, const2),
            pl.BlockSpec(lb_b.shape, const2),
            pl.BlockSpec(gsum.shape, const2),
        ],
        out_specs=[
            pl.BlockSpec((None, tm, d), cat),
            pl.BlockSpec((None, tm, d), cat),
            pl.BlockSpec((None, tm, d), cat),
            pl.BlockSpec((None, tm, d), cat),
            pl.BlockSpec((None, tm, d), cat),
            pl.BlockSpec((None, tm, d), lat),
            pl.BlockSpec((None, tm, d), lat),
            pl.BlockSpec((None, tm, d), lat),
            pl.BlockSpec((None, tm, d), lat),
            pl.BlockSpec((None, tm, 2 * d), lat),
        ],
        out_shape=[bf(s, d), bf(s, d), bf(s, d),
                   jax.ShapeDtypeStruct((bsz, s, d), F32), jax.ShapeDtypeStruct((bsz, s, d), F32),
                   bf(t, d), bf(t, d), bf(t, d), bf(t, d), bf(t, 2 * d)],
        compiler_params=pltpu.CompilerParams(
            dimension_semantics=("arbitrary", "arbitrary"), vmem_limit_bytes=VMEM_LIMIT),
        name="proj",
    )(x, ctx, mod, norm_gain, w_bf, q_gain, k_gain, cos_t, sin_a, sin_b, lb_f, lb_b, gsum)


def _attn_kernel(bounded_ref, q_ref, k_ref, v_ref, lq1_ref, lk1_ref, lq2_ref, lk2_ref, sg_ref, o_ref,
                 e_scr, c_scr, vt_scr, *, tq):
    s = k_ref.shape[0]
    n_tiles = q_ref.shape[0] // tq
    chunks = [(st, min(KEY_CHUNK, s - st)) for st in range(0, s, KEY_CHUNK)]
    vt_scr[...] = v_ref[...].astype(F32).T.astype(BF16)
    lam = (jnp.exp(jnp.sum(lq1_ref[...] * lk1_ref[...], axis=-1, keepdims=True))
           - jnp.exp(jnp.sum(lq2_ref[...] * lk2_ref[...], axis=-1, keepdims=True)) + LAM_INIT)

    def map_queries(i):
        q = q_ref[pl.ds(pl.multiple_of(i * tq, tq), tq), :]
        lane = lax.broadcasted_iota(jnp.int32, q.shape, 1)
        zero = jnp.zeros_like(q)
        return jnp.where(lane < MAP_W, q, zero), jnp.where(lane >= MAP_W, q, zero)

    def scores(kc, qm):
        return lax.dot_general(kc, qm, _NT, preferred_element_type=F32)

    def stage_a_bounded(i, slot):
        qs = map_queries(i)
        sums = [jnp.zeros((1, tq), F32), jnp.zeros((1, tq), F32)]
        for st, sz in chunks:
            kc = k_ref[st:st + sz, :]
            for m in range(2):
                e = jnp.exp2(scores(kc, qs[m]))
                sums[m] = sums[m] + jnp.sum(e, axis=0, keepdims=True)
                e_scr[slot, m, st:st + sz, :] = e.astype(BF16)
        c_scr[slot, 0:1, :] = 1.0 / sums[0]
        c_scr[slot, 1:2, :] = lam / sums[1]

    def stage_a_general(i, slot):
        qs = map_queries(i)
        k = k_ref[...]
        for m in range(2):
            sm = scores(k, qs[m])
            e = jnp.exp2(sm - jnp.max(sm, axis=0, keepdims=True))
            c_scr[slot, m:m + 1, :] = (lam if m else 1.0) / jnp.sum(e, axis=0, keepdims=True)
            e_scr[slot, m] = e.astype(BF16)

    def stage_b(i, slot):
        c1 = c_scr[slot, 0:1, :].astype(BF16)
        c2 = c_scr[slot, 1:2, :].astype(BF16)
        acc = jnp.zeros((vt_scr.shape[0], tq), F32)
        for st, sz in chunks:
            w = e_scr[slot, 0, st:st + sz, :] * c1 - e_scr[slot, 1, st:st + sz, :] * c2
            acc = acc + jnp.dot(vt_scr[:, st:st + sz], w, preferred_element_type=F32)
        o = acc.T
        ms = jnp.mean(o * o, axis=-1, keepdims=True)
        o = o * lax.rsqrt(ms + EPS) * sg_ref[...] * (1.0 - LAM_INIT)
        o_ref[pl.ds(pl.multiple_of(i * tq, tq), tq), :] = o.astype(o_ref.dtype)

    @pl.when(bounded_ref[0] != 0)
    def _():
        stage_a_bounded(0, 0)

        def pair(j, _):
            i = 2 * j
            stage_b(i, 0)
            stage_a_bounded(i + 1, 1)
            stage_b(i + 1, 1)
            stage_a_bounded(i + 2, 0)
            return 0

        lax.fori_loop(0, n_tiles // 2 - 1, pair, 0)
        stage_b(n_tiles - 2, 0)
        stage_a_bounded(n_tiles - 1, 1)
        stage_b(n_tiles - 1, 1)

    @pl.when(bounded_ref[0] == 0)
    def _():
        def tile(i, _):
            stage_a_general(i, 0)
            stage_b(i, 0)
            return 0

        lax.fori_loop(0, n_tiles, tile, 0)


def _attn_call(bounded, q, k, v, lq1, lk1, lq2, lk2, subln_gain):
    bsz, t, d = q.shape
    s = k.shape[1]
    tq = ATTN_ROWS
    assert t % (2 * tq) == 0
    const = lambda b, h: (0, 0)
    head = lambda b, h: (b, 0, h)
    return pl.pallas_call(
        functools.partial(_attn_kernel, tq=tq),
        grid=(bsz, HEADS),
        in_specs=[
            pl.BlockSpec(memory_space=pltpu.SMEM),
            pl.BlockSpec((None, t, HEAD_W), head),
            pl.BlockSpec((None, s, HEAD_W), head),
            pl.BlockSpec((None, s, HEAD_W), head),
            pl.BlockSpec((1, MAP_W), const), pl.BlockSpec((1, MAP_W), const),
            pl.BlockSpec((1, MAP_W), const), pl.BlockSpec((1, MAP_W), const),
            pl.BlockSpec((1, HEAD_W), const),
        ],
        out_specs=pl.BlockSpec((None, t, HEAD_W), head),
        out_shape=jax.ShapeDtypeStruct((bsz, t, d), BF16),
        scratch_shapes=[pltpu.VMEM((2, 2, s, tq), BF16), pltpu.VMEM((2, 8, tq), F32),
                        pltpu.VMEM((HEAD_W, s), BF16)],
        compiler_params=pltpu.CompilerParams(
            dimension_semantics=("arbitrary", "arbitrary"), vmem_limit_bytes=VMEM_LIMIT),
        name="attn",
    )(bounded, q, k, v, lq1, lk1, lq2, lk2, subln_gain)


def _prefix_matrix(c, rev):
    t = np.arange(c)
    m = (t[None, :] >= t[:, None]) if rev else (t[None, :] <= t[:, None])
    return jnp.asarray(m.astype(np.float32), BF16)


def _split2(g):
    hi = g.astype(BF16)
    mid = (g - hi.astype(F32)).astype(BF16)
    return jnp.concatenate([hi, mid], axis=1)


def _cum_decay(l_ref, g, b_scr, rev):
    c, w = g.shape
    e = jnp.dot(l_ref[...], _split2(g), preferred_element_type=F32)
    b = e[:, 0:w] + e[:, w:2 * w]
    b_scr[...] = b
    tot = b_scr[0:1, :] if rev else b_scr[c - 1:c, :]
    return b, tot


def _state_update(st, v, kk, e_to_end, e_total):
    kd = (kk * e_to_end).astype(BF16)
    vt = v.astype(F32).T.astype(BF16)
    return st * e_total + jnp.dot(vt, kd, preferred_element_type=F32)


def _ctx_chunk(st, g, v, l_ref, b_scr, rev):
    b, tot = _cum_decay(l_ref, g, b_scr, rev)
    kk = 1.0 - jnp.exp2(g)
    return _state_update(st, v, kk, jnp.exp2(tot - b), jnp.exp2(tot))


def _level_operand(m, qf, kk, eg, b, b_scr, rev):
    c, w = b.shape
    ref = m if rev else m - 1
    bcast = lambda r, n: jnp.broadcast_to(b_scr[r:r + 1, :], (n, w))
    if m >= 8:
        pieces = []
        for j in range(c // (2 * m)):
            lo = slice(j * 2 * m, j * 2 * m + m)
            hi = slice(j * 2 * m + m, (j + 1) * 2 * m)
            r = bcast(j * 2 * m + ref, m)
            if rev:
                pieces += [qf[lo] * jnp.exp2(b[lo] - r), kk[hi] * jnp.exp2(r - b[hi])]
            else:
                pieces += [kk[lo] * jnp.exp2(r - b[lo]), qf[hi] * jnp.exp2(b[hi] - r)]
        return jnp.concatenate(pieces, axis=0).astype(BF16)
    row = lax.broadcasted_iota(jnp.int32, (c, w), 0)
    upper = (row & m) != 0
    q_rows = jnp.logical_not(upper) if rev else upper
    if m == 1:
        f = jnp.where(q_rows, eg, 1.0)
    else:
        if m == 4:
            r = jnp.concatenate([bcast(8 * j + ref, 8) for j in range(c // 8)], axis=0)
        else:
            sub = lax.broadcasted_iota(jnp.int32, (8, w), 0)
            r = jnp.concatenate([jnp.where(sub < 4, bcast(8 * j + ref, 8), bcast(8 * j + 4 + ref, 8))
                                 for j in range(c // 8)], axis=0)
        f = jnp.exp2(-jnp.abs(b - r))
    return (jnp.where(q_rows, qf, kk) * f).astype(BF16)


def _lat_chunk(st, g, q, v, l_ref, b_scr, rev):
    c, w = g.shape
    h = c // 2
    n_lvl = int(math.log2(c))
    b, tot = _cum_decay(l_ref, g, b_scr, rev)
    eg = jnp.exp2(g)
    kk = 1.0 - eg
    qf = q.astype(F32)
    st_new = _state_update(st, v, kk, jnp.exp2(tot - b), jnp.exp2(tot))
    qd = (qf * jnp.exp2(b)).astype(BF16)
    o = lax.dot_general(qd, st.astype(BF16), _NT, preferred_element_type=F32)
    o = o + jnp.sum(qf * kk, axis=-1, keepdims=True) * v.astype(F32)

    x = _level_operand(h, qf, kk, eg, b, b_scr, rev)
    p = lax.dot_general(x, x, _NT, preferred_element_type=F32)
    a_off = p[0:h, h:c] if rev else p[h:c, 0:h]
    ti = lax.broadcasted_iota(jnp.int32, (h, h), 0)
    si = lax.broadcasted_iota(jnp.int32, (h, h), 1)
    differ = ti ^ si
    a0 = a1 = None
    for lvl in reversed(range(n_lvl - 1)):
        m = 1 << lvl
        x = _level_operand(m, qf, kk, eg, b, b_scr, rev)
        p = lax.dot_general(x, x, _NT, preferred_element_type=F32)
        if a0 is None:
            a0, a1 = p[0:h, 0:h], p[h:c, h:c]
        else:
            same_block = differ < 2 * m
            a0 = jnp.where(same_block, p[0:h, 0:h], a0)
            a1 = jnp.where(same_block, p[h:c, h:c], a1)
    valid = (ti < si) if rev else (ti > si)
    a0 = jnp.where(valid, a0, 0.0).astype(BF16)
    a1 = jnp.where(valid, a1, 0.0).astype(BF16)
    a_off = a_off.astype(BF16)
    if rev:
        o_lo = jnp.dot(jnp.concatenate([a0, a_off], axis=1), v, preferred_element_type=F32)
        o_hi = jnp.dot(a1, v[h:c], preferred_element_type=F32)
    else:
        o_lo = jnp.dot(a0, v[0:h], preferred_element_type=F32)
        o_hi = jnp.dot(jnp.concatenate([a_off, a1], axis=1), v, preferred_element_type=F32)
    return st_new, o + jnp.concatenate([o_lo, o_hi], axis=0)


def _hgrn_kernel(q_ref, v_ref, gf_ref, gb_ref, lf_ref, lb_ref, hg_ref, o_ref, of_scr, ob_scr, bf_scr, bb_scr,
                 *, lc, c):
    t = q_ref.shape[0]
    dk = gf_ref.shape[1]
    dv = v_ref.shape[1]
    n_ctx = lc // c
    n_lat = t // c
    st0 = jnp.zeros((dv, dk), F32)

    def ctx(j, carry):
        st_f, st_b = carry
        rf = pl.multiple_of(j * c, c)
        rb = pl.multiple_of((n_ctx - 1 - j) * c, c)
        st_f = _ctx_chunk(st_f, gf_ref[pl.ds(rf, c), :], v_ref[pl.ds(rf, c), :], lf_ref, bf_scr.at[0], False)
        st_b = _ctx_chunk(st_b, gb_ref[pl.ds(rb, c), :], v_ref[pl.ds(rb, c), :], lb_ref, bb_scr.at[0], True)
        return st_f, st_b

    carry = lax.fori_loop(0, n_ctx, ctx, (st0, st0))

    def lat(jj, carry):
        st_f, st_b = carry
        for u in range(LAT_UNROLL):
            j = jj * LAT_UNROLL + u
            rf = pl.multiple_of(j * c, c)
            rb = pl.multiple_of((n_lat - 1 - j) * c, c)
            sf = pl.multiple_of(lc + j * c, c)
            sb = pl.multiple_of(lc + (n_lat - 1 - j) * c, c)
            st_f, o_f = _lat_chunk(st_f, gf_ref[pl.ds(sf, c), :], q_ref[pl.ds(rf, c), :],
                                   v_ref[pl.ds(sf, c), :], lf_ref, bf_scr.at[u], False)
            of_scr[pl.ds(rf, c), :] = o_f
            st_b, o_b = _lat_chunk(st_b, gb_ref[pl.ds(sb, c), :], q_ref[pl.ds(rb, c), :],
                                   v_ref[pl.ds(sb, c), :], lb_ref, bb_scr.at[u], True)
            ob_scr[pl.ds(rb, c), :] = o_b
        return st_f, st_b

    lax.fori_loop(0, n_lat // LAT_UNROLL, lat, carry)

    def combine(j, _):
        r = pl.multiple_of(j * c, c)
        o = of_scr[pl.ds(r, c), :] + ob_scr[pl.ds(r, c), :]
        ms = jnp.mean(o * o, axis=-1, keepdims=True)
        o_ref[pl.ds(r, c), :] = (o * lax.rsqrt(ms + EPS) * hg_ref[...]).astype(o_ref.dtype)
        return 0

    lax.fori_loop(0, n_lat, combine, 0)


def _hgrn_call(hq, hv, gf, gb, hgrn_gain, lc):
    bsz, t, d = hq.shape
    s = hv.shape[1]
    c = CHUNK
    assert lc % c == 0 and t % c == 0
    lf = _prefix_matrix(c, False)
    lb = _prefix_matrix(c, True)
    const = lambda b, h: (0, 0)
    kern = functools.partial(_hgrn_kernel, lc=lc, c=c)
    return pl.pallas_call(
        kern,
        grid=(bsz, HEADS),
        in_specs=[
            pl.BlockSpec((None, t, HEAD_W), lambda b, h: (b, 0, h)),
            pl.BlockSpec((None, s, HEAD_W), lambda b, h: (b, 0, h)),
            pl.BlockSpec((None, s, HEAD_W), lambda b, h: (b, 0, h)),
            pl.BlockSpec((None, s, HEAD_W), lambda b, h: (b, 0, h)),
            pl.BlockSpec(lf.shape, const), pl.BlockSpec(lb.shape, const),
            pl.BlockSpec((1, HEAD_W), const),
        ],
        out_specs=pl.BlockSpec((None, t, HEAD_W), lambda b, h: (b, 0, h)),
        out_shape=jax.ShapeDtypeStruct((bsz, t, d), BF16),
        scratch_shapes=[pltpu.VMEM((t, HEAD_W), F32), pltpu.VMEM((t, HEAD_W), F32),
                        pltpu.VMEM((LAT_UNROLL, c, HEAD_W), F32), pltpu.VMEM((LAT_UNROLL, c, HEAD_W), F32)],
        compiler_params=pltpu.CompilerParams(
            dimension_semantics=("arbitrary", "arbitrary"), vmem_limit_bytes=VMEM_LIMIT),
        name="hgrn",
    )(hq, hv, gf, gb, lf, lb, hgrn_gain)


def _merge_kernel(oa_ref, oh_ref, za_ref, zh_ref, mg_ref, x_ref, mod_ref, wba_ref, wbh_ref, wo_ref, out_ref, *, d):
    b = pl.program_id(0)
    ya = oa_ref[...] * za_ref[...]
    yh = oh_ref[...] * zh_ref[...]
    ta = jnp.dot(ya, wba_ref[...], preferred_element_type=F32)
    th = jnp.dot(yh, wbh_ref[...], preferred_element_type=F32)
    mix = mg_ref[:, 0:d].astype(F32) * ta + mg_ref[:, d:2 * d].astype(F32) * th
    y = jnp.dot(mix.astype(BF16), wo_ref[...], preferred_element_type=F32)
    gate = mod_ref[pl.ds(b, 1), 2 * d:3 * d]
    out_ref[...] = x_ref[...] + gate * y


def _merge_call(oa, oh, za, zh, mg, x, mod, wba, wbh, wo):
    bsz, t, d = x.shape
    tm = MERGE_ROWS
    tok = lambda b, i: (b, i, 0)
    const = lambda b, i: (0, 0)
    return pl.pallas_call(
        functools.partial(_merge_kernel, d=d),
        grid=(bsz, t // tm),
        in_specs=[
            pl.BlockSpec((None, tm, d), tok), pl.BlockSpec((None, tm, d), tok),
            pl.BlockSpec((None, tm, d), tok), pl.BlockSpec((None, tm, d), tok),
            pl.BlockSpec((None, tm, 2 * d), tok), pl.BlockSpec((None, tm, d), tok),
            pl.BlockSpec(mod.shape, const),
            pl.BlockSpec((d, d), const), pl.BlockSpec((d, d), const), pl.BlockSpec((d, d), const),
        ],
        out_specs=pl.BlockSpec((None, tm, d), tok),
        out_shape=jax.ShapeDtypeStruct((bsz, t, d), F32),
        compiler_params=pltpu.CompilerParams(
            dimension_semantics=("arbitrary", "arbitrary"), vmem_limit_bytes=VMEM_LIMIT),
        name="merge",
    )(oa, oh, za, zh, mg, x, mod, wba, wbh, wo)


def _rope_tables(t, lc):
    rows = t // GRID_W
    row = jnp.repeat(jnp.arange(rows, dtype=F32), GRID_W)
    col = jnp.tile(jnp.arange(GRID_W, dtype=F32), rows)
    half = MAP_W // 2
    inv = ROPE_BASE ** (-jnp.arange(0, half, 2, dtype=F32) / half)
    ar = row[:, None] * inv[None, :]
    ac = col[:, None] * inv[None, :]
    ang = jnp.concatenate([ar, ar, ac, ac], axis=-1)
    ang = jnp.concatenate([ang, ang], axis=-1)
    cos, sin = jnp.cos(ang), jnp.sin(ang)
    first = (jnp.arange(HEAD_W) % (half)) < (half // 2)
    sin_a = jnp.where(first[None, :], -sin, 0.0)
    sin_b = jnp.where(first[None, :], 0.0, sin)
    pad = lambda a, v: jnp.concatenate([jnp.full((lc, HEAD_W), v, F32), a], axis=0)
    return pad(cos, 1.0), pad(sin_a, 0.0), pad(sin_b, 0.0)


def kernel(x, c, ctx, c_ctx, w_mod, b_mod, norm_gain, w_in, q_norm_gain, k_norm_gain, lambda_q1, lambda_k1,
           lambda_q2, lambda_k2, subln_gain, hgrn_lb_fwd, hgrn_lb_bwd, hgrn_norm_gain, w_br_attn, w_br_hgrn,
           w_out):
    bsz, t, d = x.shape
    lc = ctx.shape[1]
    assert w_mod.shape[0] == 1, "single-layer problem"

    mod_rows = ((bsz + 1 + 7) // 8) * 8
    cc = jnp.concatenate([c, c_ctx[None, :], jnp.zeros((mod_rows - bsz - 1, d), F32)], axis=0)
    mod = _mod_call(cc, w_mod[0], b_mod[0][None, :])

    cos_t, sin_a, sin_b = _rope_tables(t, lc)
    q_gain = jnp.tile(q_norm_gain[0], 2)[None, :] * (MAP_W ** -0.5 * LOG2E)
    k_gain = jnp.tile(k_norm_gain[0], 2)[None, :]
    score_bound = (1.03 * MAP_W) * jnp.max(jnp.abs(q_gain)) * jnp.max(jnp.abs(k_gain))
    bounded = (score_bound <= MAX_SOFTMAX_SHIFT).astype(jnp.int32).reshape(1)
    lane = np.arange(256)
    gsum = jnp.asarray((lane[:, None] // MAP_W) == (lane[None, :] // MAP_W), BF16)
    k_a, v_a, v_h, g_f, g_b, q_a, q_h, z_a, z_h, m_g = _proj_call(
        x, ctx, mod, norm_gain, w_in[0].astype(BF16), q_gain, k_gain, cos_t, sin_a, sin_b,
        hgrn_lb_fwd, hgrn_lb_bwd, gsum)

    o_a = _attn_call(bounded, q_a, k_a, v_a, lambda_q1, lambda_k1, lambda_q2, lambda_k2, subln_gain)
    o_h = _hgrn_call(q_h, v_h, g_f, g_b, hgrn_norm_gain, lc)
    return _merge_call(o_a, o_h, z_a, z_h, m_g, x, mod,
                       w_br_attn[0].astype(BF16), w_br_hgrn[0].astype(BF16), w_out[0].astype(BF16))
```

```python
import functools
import math

import jax
import jax.numpy as jnp
import numpy as np
from jax import lax
from jax.experimental import pallas as pl
from jax.experimental.pallas import tpu as pltpu

F32 = jnp.float32
BF16 = jnp.bfloat16

HEADS = 8
HEAD_W = 128
MAP_W = 64
GRID_W = 64
ROPE_BASE = 10000.0
EPS = 1e-6
LOG2E = 1.4426950408889634
MAX_SOFTMAX_SHIFT = 50.0
LAM_INIT = 0.8 - 0.6 * math.exp(-0.3 * 0)

PROJ_ROWS = 256
ATTN_ROWS = 256
KEY_CHUNK = 1024
MERGE_ROWS = 512
CHUNK = 256
LAT_UNROLL = 4
MID_BLOCK = 32
MID_MAX_EXPONENT = 100.0
VMEM_LIMIT = 56 * 1024 * 1024

G_ATTN_K, G_ATTN_V, G_HGRN_I, G_F_FWD, G_F_BWD, G_ATTN_Q, G_HGRN_Q, G_ATTN_Z, G_HGRN_Z, G_MERGE = range(10)

_NT = (((1,), (1,)), ((), ()))


def _sigmoid(a):
    return 1.0 / (1.0 + jnp.exp(-a))


def _mod_kernel(c_ref, w_ref, b_ref, o_ref):
    a = c_ref[...]
    s = a * _sigmoid(a)
    o_ref[...] = jnp.dot(s, w_ref[...], precision=lax.Precision.HIGHEST,
                         preferred_element_type=F32) + b_ref[...]


def _mod_call(cc, w_mod, b_mod):
    rows, d = cc.shape
    n = w_mod.shape[1]
    bn = 1024
    return pl.pallas_call(
        _mod_kernel,
        grid=(n // bn,),
        in_specs=[pl.BlockSpec((rows, d), lambda j: (0, 0)),
                  pl.BlockSpec((d, bn), lambda j: (0, j)),
                  pl.BlockSpec((1, bn), lambda j: (0, j))],
        out_specs=pl.BlockSpec((rows, bn), lambda j: (0, j)),
        out_shape=jax.ShapeDtypeStruct((rows, n), F32),
        name="mod",
    )(cc, w_mod, b_mod)


def _proj_kernel(x_ref, ctx_ref, mod_ref, ng_ref, w_ref, qg_ref, kg_ref, cos_ref, sa_ref, sb_ref,
                 lbf_ref, lbb_ref, gsum_ref,
                 k_out, v_out, hv_out, gf_out, gb_out, q_out, hq_out, za_out, zh_out, mg_out,
                 *, n_ctx_tiles, ctx_row, d):
    b = pl.program_id(0)
    i = pl.program_id(1)
    is_ctx = i < n_ctx_tiles

    u = jnp.where(is_ctx, ctx_ref[...], x_ref[...])
    row = jnp.where(is_ctx, ctx_row, b)
    shift = mod_ref[pl.ds(row, 1), 0:d]
    scale = mod_ref[pl.ds(row, 1), d:2 * d]
    ms = jnp.mean(u * u, axis=-1, keepdims=True)
    y = u * lax.rsqrt(ms + EPS) * ng_ref[...]
    hb = (y * (1.0 + scale) + shift).astype(BF16)

    def group(g, width=1024):
        return jnp.dot(hb, w_ref[:, g * 1024:g * 1024 + width], preferred_element_type=F32)

    def qk_epilogue(p, gain_ref, out_ref):
        p2 = (p * p).astype(BF16)
        for cch in range(d // 256):
            ss = jnp.dot(p2[:, cch * 256:(cch + 1) * 256], gsum_ref[...], preferred_element_type=F32)
            inv = lax.rsqrt(ss * (1.0 / MAP_W) + EPS)
            for hh in range(2):
                lo = cch * 256 + hh * HEAD_W
                un = p[:, lo:lo + HEAD_W] * inv[:, hh * HEAD_W:(hh + 1) * HEAD_W] * gain_ref[...]
                r = (un * cos_ref[...] + pltpu.roll(un, HEAD_W - 16, 1) * sa_ref[...]
                     + pltpu.roll(un, 16, 1) * sb_ref[...])
                out_ref[:, lo:lo + HEAD_W] = r.astype(out_ref.dtype)

    def lower_bound(ref):
        p = ref[...]
        e = jnp.exp(p - jnp.max(p, axis=0, keepdims=True))
        return e[0:1] / jnp.sum(e, axis=0, keepdims=True)

    def log_forget(a, lb):
        return jnp.log(lb + (1.0 - lb) * _sigmoid(a)) * LOG2E

    qk_epilogue(group(G_ATTN_K), kg_ref, k_out)
    v_out[...] = group(G_ATTN_V).astype(BF16)
    hv_out[...] = group(G_HGRN_I).astype(BF16)
    gf_out[...] = log_forget(group(G_F_FWD), lower_bound(lbf_ref))
    gb_out[...] = log_forget(group(G_F_BWD), lower_bound(lbb_ref))

    @pl.when(jnp.logical_not(is_ctx))
    def _():
        qk_epilogue(group(G_ATTN_Q), qg_ref, q_out)
        a = group(G_HGRN_Q)
        hq_out[...] = (a * _sigmoid(a)).astype(BF16)
        a = group(G_ATTN_Z)
        za_out[...] = (a * _sigmoid(a)).astype(BF16)
        a = group(G_HGRN_Z)
        zh_out[...] = (a * _sigmoid(a)).astype(BF16)
        mg_out[:, 0:1024] = _sigmoid(group(G_MERGE)).astype(BF16)
        mg_out[:, 1024:2048] = _sigmoid(group(G_MERGE + 1)).astype(BF16)


def _proj_call(x, ctx, mod, norm_gain, w_bf, q_gain, k_gain, cos_t, sin_a, sin_b, lb_f, lb_b, gsum):
    bsz, t, d = x.shape
    lc = ctx.shape[1]
    tm = PROJ_ROWS
    assert lc % tm == 0 and t % tm == 0
    nct = lc // tm
    s = lc + t
    n_tiles = s // tm
    lat = lambda b, i: (b, jnp.maximum(i - nct, 0), 0)
    cat = lambda b, i: (b, i, 0)
    const2 = lambda b, i: (0, 0)
    kern = functools.partial(_proj_kernel, n_ctx_tiles=nct, ctx_row=bsz, d=d)
    bf = lambda n, w: jax.ShapeDtypeStruct((bsz, n, w), BF16)
    return pl.pallas_call(
        kern,
        grid=(bsz, n_tiles),
        in_specs=[
            pl.BlockSpec((None, tm, d), lat),
            pl.BlockSpec((None, tm, d), lambda b, i: (b, jnp.minimum(i, nct - 1), 0)),
            pl.BlockSpec(mod.shape, const2),
            pl.BlockSpec((1, d), const2),
            pl.BlockSpec(w_bf.shape, const2, pipeline_mode=pl.Buffered(1)),
            pl.BlockSpec((1, HEAD_W), const2),
            pl.BlockSpec((1, HEAD_W), const2),
            pl.BlockSpec((tm, HEAD_W), lambda b, i: (i, 0)),
            pl.BlockSpec((tm, HEAD_W), lambda b, i: (i, 0)),
            pl.BlockSpec((tm, HEAD_W), lambda b, i: (i, 0)),
            pl.BlockSpec(lb_f.shape, const2),
            pl.BlockSpec(lb_b.shape, const2),
            pl.BlockSpec(gsum.shape, const2),
        ],
        out_specs=[
            pl.BlockSpec((None, tm, d), cat),
            pl.BlockSpec((None, tm, d), cat),
            pl.BlockSpec((None, tm, d), cat),
            pl.BlockSpec((None, tm, d), cat),
            pl.BlockSpec((None, tm, d), cat),
            pl.BlockSpec((None, tm, d), lat),
            pl.BlockSpec((None, tm, d), lat),
            pl.BlockSpec((None, tm, d), lat),
            pl.BlockSpec((None, tm, d), lat),
            pl.BlockSpec((None, tm, 2 * d), lat),
        ],
        out_shape=[bf(s, d), bf(s, d), bf(s, d),
                   jax.ShapeDtypeStruct((bsz, s, d), F32), jax.ShapeDtypeStruct((bsz, s, d), F32),
                   bf(t, d), bf(t, d), bf(t, d), bf(t, d), bf(t, 2 * d)],
        compiler_params=pltpu.CompilerParams(
            dimension_semantics=("arbitrary", "arbitrary"), vmem_limit_bytes=VMEM_LIMIT),
        name="proj",
    )(x, ctx, mod, norm_gain, w_bf, q_gain, k_gain, cos_t, sin_a, sin_b, lb_f, lb_b, gsum)


def _attn_kernel(bounded_ref, q_ref, k_ref, v_ref, lq1_ref, lk1_ref, lq2_ref, lk2_ref, sg_ref, o_ref,
                 e_scr, c_scr, vt_scr, *, tq):
    s = k_ref.shape[0]
    n_tiles = q_ref.shape[0] // tq
    chunks = [(st, min(KEY_CHUNK, s - st)) for st in range(0, s, KEY_CHUNK)]
    vt_scr[...] = v_ref[...].astype(F32).T.astype(BF16)
    lam = (jnp.exp(jnp.sum(lq1_ref[...] * lk1_ref[...], axis=-1, keepdims=True))
           - jnp.exp(jnp.sum(lq2_ref[...] * lk2_ref[...], axis=-1, keepdims=True)) + LAM_INIT)

    def map_queries(i):
        q = q_ref[pl.ds(pl.multiple_of(i * tq, tq), tq), :]
        lane = lax.broadcasted_iota(jnp.int32, q.shape, 1)
        zero = jnp.zeros_like(q)
        return jnp.where(lane < MAP_W, q, zero), jnp.where(lane >= MAP_W, q, zero)

    def scores(kc, qm):
        return lax.dot_general(kc, qm, _NT, preferred_element_type=F32)

    def stage_a_bounded(i, slot):
        qs = map_queries(i)
        sums = [jnp.zeros((1, tq), F32), jnp.zeros((1, tq), F32)]
        for st, sz in chunks:
            kc = k_ref[st:st + sz, :]
            for m in range(2):
                e = jnp.exp2(scores(kc, qs[m]))
                sums[m] = sums[m] + jnp.sum(e, axis=0, keepdims=True)
                e_scr[slot, m, st:st + sz, :] = e.astype(BF16)
        c_scr[slot, 0:1, :] = 1.0 / sums[0]
        c_scr[slot, 1:2, :] = lam / sums[1]

    def stage_a_general(i, slot):
        qs = map_queries(i)
        k = k_ref[...]
        for m in range(2):
            sm = scores(k, qs[m])
            e = jnp.exp2(sm - jnp.max(sm, axis=0, keepdims=True))
            c_scr[slot, m:m + 1, :] = (lam if m else 1.0) / jnp.sum(e, axis=0, keepdims=True)
            e_scr[slot, m] = e.astype(BF16)

    def stage_b(i, slot):
        c1 = c_scr[slot, 0:1, :].astype(BF16)
        c2 = c_scr[slot, 1:2, :].astype(BF16)
        acc = jnp.zeros((vt_scr.shape[0], tq), F32)
        for st, sz in chunks:
            w = e_scr[slot, 0, st:st + sz, :] * c1 - e_scr[slot, 1, st:st + sz, :] * c2
            acc = acc + jnp.dot(vt_scr[:, st:st + sz], w, preferred_element_type=F32)
        o = acc.T
        ms = jnp.mean(o * o, axis=-1, keepdims=True)
        o = o * lax.rsqrt(ms + EPS) * sg_ref[...] * (1.0 - LAM_INIT)
        o_ref[pl.ds(pl.multiple_of(i * tq, tq), tq), :] = o.astype(o_ref.dtype)

    @pl.when(bounded_ref[0] != 0)
    def _():
        stage_a_bounded(0, 0)

        def pair(j, _):
            i = 2 * j
            stage_b(i, 0)
            stage_a_bounded(i + 1, 1)
            stage_b(i + 1, 1)
            stage_a_bounded(i + 2, 0)
            return 0

        lax.fori_loop(0, n_tiles // 2 - 1, pair, 0)
        stage_b(n_tiles - 2, 0)
        stage_a_bounded(n_tiles - 1, 1)
        stage_b(n_tiles - 1, 1)

    @pl.when(bounded_ref[0] == 0)
    def _():
        def tile(i, _):
            stage_a_general(i, 0)
            stage_b(i, 0)
            return 0

        lax.fori_loop(0, n_tiles, tile, 0)


def _attn_call(bounded, q, k, v, lq1, lk1, lq2, lk2, subln_gain):
    bsz, t, d = q.shape
    s = k.shape[1]
    tq = ATTN_ROWS
    assert t % (2 * tq) == 0
    const = lambda b, h: (0, 0)
    head = lambda b, h: (b, 0, h)
    return pl.pallas_call(
        functools.partial(_attn_kernel, tq=tq),
        grid=(bsz, HEADS),
        in_specs=[
            pl.BlockSpec(memory_space=pltpu.SMEM),
            pl.BlockSpec((None, t, HEAD_W), head),
            pl.BlockSpec((None, s, HEAD_W), head),
            pl.BlockSpec((None, s, HEAD_W), head),
            pl.BlockSpec((1, MAP_W), const), pl.BlockSpec((1, MAP_W), const),
            pl.BlockSpec((1, MAP_W), const), pl.BlockSpec((1, MAP_W), const),
            pl.BlockSpec((1, HEAD_W), const),
        ],
        out_specs=pl.BlockSpec((None, t, HEAD_W), head),
        out_shape=jax.ShapeDtypeStruct((bsz, t, d), BF16),
        scratch_shapes=[pltpu.VMEM((2, 2, s, tq), BF16), pltpu.VMEM((2, 8, tq), F32),
                        pltpu.VMEM((HEAD_W, s), BF16)],
        compiler_params=pltpu.CompilerParams(
            dimension_semantics=("arbitrary", "arbitrary"), vmem_limit_bytes=VMEM_LIMIT),
        name="attn",
    )(bounded, q, k, v, lq1, lk1, lq2, lk2, subln_gain)


def _prefix_matrix(c, rev):
    t = np.arange(c)
    m = (t[None, :] >= t[:, None]) if rev else (t[None, :] <= t[:, None])
    return jnp.asarray(m.astype(np.float32), BF16)


def _split2(g):
    hi = g.astype(BF16)
    mid = (g - hi.astype(F32)).astype(BF16)
    return jnp.concatenate([hi, mid], axis=1)


def _cum_decay(l_ref, g, b_scr, rev):
    c, w = g.shape
    e = jnp.dot(l_ref[...], _split2(g), preferred_element_type=F32)
    b = e[:, 0:w] + e[:, w:2 * w]
    b_scr[...] = b
    tot = b_scr[0:1, :] if rev else b_scr[c - 1:c, :]
    return b, tot


def _state_update(st, v, kk, e_to_end, e_total):
    kd = (kk * e_to_end).astype(BF16)
    vt = v.astype(F32).T.astype(BF16)
    return st * e_total + jnp.dot(vt, kd, preferred_element_type=F32)


def _ctx_chunk(st, g, v, l_ref, b_scr, rev):
    b, tot = _cum_decay(l_ref, g, b_scr, rev)
    kk = 1.0 - jnp.exp2(g)
    return _state_update(st, v, kk, jnp.exp2(tot - b), jnp.exp2(tot))


def _level_operand(m, qf, kk, eg, b, b_scr, rev):
    c, w = b.shape
    ref = m if rev else m - 1
    bcast = lambda r, n: jnp.broadcast_to(b_scr[r:r + 1, :], (n, w))
    if m >= 8:
        pieces = []
        for j in range(c // (2 * m)):
            lo = slice(j * 2 * m, j * 2 * m + m)
            hi = slice(j * 2 * m + m, (j + 1) * 2 * m)
            r = bcast(j * 2 * m + ref, m)
            if rev:
                pieces += [qf[lo] * jnp.exp2(b[lo] - r), kk[hi] * jnp.exp2(r - b[hi])]
            else:
                pieces += [kk[lo] * jnp.exp2(r - b[lo]), qf[hi] * jnp.exp2(b[hi] - r)]
        return jnp.concatenate(pieces, axis=0).astype(BF16)
    row = lax.broadcasted_iota(jnp.int32, (c, w), 0)
    upper = (row & m) != 0
    q_rows = jnp.logical_not(upper) if rev else upper
    if m == 1:
        f = jnp.where(q_rows, eg, 1.0)
    else:
        if m == 4:
            r = jnp.concatenate([bcast(8 * j + ref, 8) for j in range(c // 8)], axis=0)
        else:
            sub = lax.broadcasted_iota(jnp.int32, (8, w), 0)
            r = jnp.concatenate([jnp.where(sub < 4, bcast(8 * j + ref, 8), bcast(8 * j + 4 + ref, 8))
                                 for j in range(c // 8)], axis=0)
        f = jnp.exp2(-jnp.abs(b - r))
    return (jnp.where(q_rows, qf, kk) * f).astype(BF16)


def _mid_operands(qf, kk, b, b_scr, rev):
    c, w = b.shape
    mid = MID_BLOCK // 2
    r = jnp.concatenate([jnp.broadcast_to(b_scr[j * MID_BLOCK + mid:j * MID_BLOCK + mid + 1, :], (MID_BLOCK, w))
                         for j in range(c // MID_BLOCK)], axis=0)
    d = b - r
    return (qf * jnp.exp2(d)).astype(BF16), (kk * jnp.exp2(-d)).astype(BF16)


def _chunk_decays(ch):
    b, tot = _cum_decay(ch["l_ref"], ch["g"], ch["b_scr"], ch["rev"])
    eg = jnp.exp2(ch["g"])
    ch.update(b=b, tot=tot, eg=eg, kk=1.0 - eg, qf=ch["q"].astype(F32))


def _chunk_state(st, ch):
    b, tot, kk, qf, v = ch["b"], ch["tot"], ch["kk"], ch["qf"], ch["v"]
    st_new = _state_update(st, v, kk, jnp.exp2(tot - b), jnp.exp2(tot))
    qd = (qf * jnp.exp2(b)).astype(BF16)
    o = lax.dot_general(qd, st.astype(BF16), _NT, preferred_element_type=F32)
    ch["o"] = o + jnp.sum(qf * kk, axis=-1, keepdims=True) * v.astype(F32)
    return st_new


def _chunk_level(ch, m, differ):
    qf, kk, eg, b, b_scr, rev = ch["qf"], ch["kk"], ch["eg"], ch["b"], ch["b_scr"], ch["rev"]
    c = b.shape[0]
    h = c // 2
    if m == "mid":
        xq, xk = _mid_operands(qf, kk, b, b_scr, rev)
        block = MID_BLOCK
    else:
        xq = xk = _level_operand(m, qf, kk, eg, b, b_scr, rev)
        block = 2 * m
    p = lax.dot_general(xq, xk, _NT, preferred_element_type=F32)
    if block == c:
        ch["a_off"] = (p[0:h, h:c] if rev else p[h:c, 0:h]).astype(BF16)
    elif block == h:
        ch["a0"], ch["a1"] = p[0:h, 0:h], p[h:c, h:c]
    else:
        same_block = differ < block
        ch["a0"] = jnp.where(same_block, p[0:h, 0:h], ch["a0"])
        ch["a1"] = jnp.where(same_block, p[h:c, h:c], ch["a1"])


def _chunk_output(ch, valid):
    v, rev = ch["v"], ch["rev"]
    c = v.shape[0]
    h = c // 2
    a0 = jnp.where(valid, ch["a0"], 0.0).astype(BF16)
    a1 = jnp.where(valid, ch["a1"], 0.0).astype(BF16)
    if rev:
        o_lo = jnp.dot(jnp.concatenate([a0, ch["a_off"]], axis=1), v, preferred_element_type=F32)
        o_hi = jnp.dot(a1, v[h:c], preferred_element_type=F32)
    else:
        o_lo = jnp.dot(a0, v[0:h], preferred_element_type=F32)
        o_hi = jnp.dot(jnp.concatenate([ch["a_off"], a1], axis=1), v, preferred_element_type=F32)
    return ch["o"] + jnp.concatenate([o_lo, o_hi], axis=0)


def _lat_group(chunks, st_f, st_b, mid_ok):
    c = chunks[0]["g"].shape[0]
    h = c // 2
    for ch in chunks:
        _chunk_decays(ch)
    for ch in chunks:
        if ch["rev"]:
            st_b = _chunk_state(st_b, ch)
        else:
            st_f = _chunk_state(st_f, ch)
    ti = lax.broadcasted_iota(jnp.int32, (h, h), 0)
    si = lax.broadcasted_iota(jnp.int32, (h, h), 1)
    differ = ti ^ si
    first_lvl = int(math.log2(MID_BLOCK)) if mid_ok else 0
    levels = [1 << lvl for lvl in reversed(range(first_lvl, int(math.log2(c))))] + (["mid"] if mid_ok else [])
    for m in levels:
        for ch in chunks:
            _chunk_level(ch, m, differ)
    outs = [_chunk_output(ch, (ti < si) if ch["rev"] else (ti > si)) for ch in chunks]
    return st_f, st_b, outs


def _hgrn_kernel(mid_ref, q_ref, v_ref, gf_ref, gb_ref, lf_ref, lb_ref, hg_ref, o_ref, of_scr, ob_scr, bf_scr,
                 bb_scr, *, lc, c):
    t = q_ref.shape[0]
    dk = gf_ref.shape[1]
    dv = v_ref.shape[1]
    n_ctx = lc // c
    n_lat = t // c
    st0 = jnp.zeros((dv, dk), F32)

    def ctx(j, carry):
        st_f, st_b = carry
        rf = pl.multiple_of(j * c, c)
        rb = pl.multiple_of((n_ctx - 1 - j) * c, c)
        st_f = _ctx_chunk(st_f, gf_ref[pl.ds(rf, c), :], v_ref[pl.ds(rf, c), :], lf_ref, bf_scr.at[0], False)
        st_b = _ctx_chunk(st_b, gb_ref[pl.ds(rb, c), :], v_ref[pl.ds(rb, c), :], lb_ref, bb_scr.at[0], True)
        return st_f, st_b

    carry = lax.fori_loop(0, n_ctx, ctx, (st0, st0))

    def latents(mid_ok):
        def lat(jj, carry):
            st_f, st_b = carry
            chunks = []
            for u in range(LAT_UNROLL):
                j = jj * LAT_UNROLL + u
                rf = pl.multiple_of(j * c, c)
                rb = pl.multiple_of((n_lat - 1 - j) * c, c)
                sf = pl.multiple_of(lc + j * c, c)
                sb = pl.multiple_of(lc + (n_lat - 1 - j) * c, c)
                chunks.append(dict(g=gf_ref[pl.ds(sf, c), :], q=q_ref[pl.ds(rf, c), :], v=v_ref[pl.ds(sf, c), :],
                                   l_ref=lf_ref, b_scr=bf_scr.at[u], rev=False, out=of_scr, row=rf))
                chunks.append(dict(g=gb_ref[pl.ds(sb, c), :], q=q_ref[pl.ds(rb, c), :], v=v_ref[pl.ds(sb, c), :],
                                   l_ref=lb_ref, b_scr=bb_scr.at[u], rev=True, out=ob_scr, row=rb))
            st_f, st_b, outs = _lat_group(chunks, st_f, st_b, mid_ok)
            for ch, o in zip(chunks, outs):
                ch["out"][pl.ds(ch["row"], c), :] = o
            return st_f, st_b

        lax.fori_loop(0, n_lat // LAT_UNROLL, lat, carry)

    @pl.when(mid_ref[0] != 0)
    def _():
        latents(True)

    @pl.when(mid_ref[0] == 0)
    def _():
        latents(False)

    o = of_scr[...] + ob_scr[...]
    ms = jnp.mean(o * o, axis=-1, keepdims=True)
    o_ref[...] = (o * lax.rsqrt(ms + EPS) * hg_ref[...]).astype(o_ref.dtype)


def _hgrn_call(mid_ok, hq, hv, gf, gb, hgrn_gain, lc):
    bsz, t, d = hq.shape
    s = hv.shape[1]
    c = CHUNK
    assert lc % c == 0 and t % c == 0
    lf = _prefix_matrix(c, False)
    lb = _prefix_matrix(c, True)
    const = lambda b, h: (0, 0)
    kern = functools.partial(_hgrn_kernel, lc=lc, c=c)
    return pl.pallas_call(
        kern,
        grid=(bsz, HEADS),
        in_specs=[
            pl.BlockSpec(memory_space=pltpu.SMEM),
            pl.BlockSpec((None, t, HEAD_W), lambda b, h: (b, 0, h)),
            pl.BlockSpec((None, s, HEAD_W), lambda b, h: (b, 0, h)),
            pl.BlockSpec((None, s, HEAD_W), lambda b, h: (b, 0, h)),
            pl.BlockSpec((None, s, HEAD_W), lambda b, h: (b, 0, h)),
            pl.BlockSpec(lf.shape, const), pl.BlockSpec(lb.shape, const),
            pl.BlockSpec((1, HEAD_W), const),
        ],
        out_specs=pl.BlockSpec((None, t, HEAD_W), lambda b, h: (b, 0, h)),
        out_shape=jax.ShapeDtypeStruct((bsz, t, d), BF16),
        scratch_shapes=[pltpu.VMEM((t, HEAD_W), F32), pltpu.VMEM((t, HEAD_W), F32),
                        pltpu.VMEM((LAT_UNROLL, c, HEAD_W), F32), pltpu.VMEM((LAT_UNROLL, c, HEAD_W), F32)],
        compiler_params=pltpu.CompilerParams(
            dimension_semantics=("arbitrary", "arbitrary"), vmem_limit_bytes=VMEM_LIMIT),
        name="hgrn",
    )(mid_ok, hq, hv, gf, gb, lf, lb, hgrn_gain)


def _merge_kernel(oa_ref, oh_ref, za_ref, zh_ref, mg_ref, x_ref, mod_ref, wba_ref, wbh_ref, wo_ref, out_ref, *, d):
    b = pl.program_id(0)
    ya = oa_ref[...] * za_ref[...]
    yh = oh_ref[...] * zh_ref[...]
    ta = jnp.dot(ya, wba_ref[...], preferred_element_type=F32)
    th = jnp.dot(yh, wbh_ref[...], preferred_element_type=F32)
    mix = mg_ref[:, 0:d].astype(F32) * ta + mg_ref[:, d:2 * d].astype(F32) * th
    y = jnp.dot(mix.astype(BF16), wo_ref[...], preferred_element_type=F32)
    gate = mod_ref[pl.ds(b, 1), 2 * d:3 * d]
    out_ref[...] = x_ref[...] + gate * y


def _merge_call(oa, oh, za, zh, mg, x, mod, wba, wbh, wo):
    bsz, t, d = x.shape
    tm = MERGE_ROWS
    tok = lambda b, i: (b, i, 0)
    const = lambda b, i: (0, 0)
    return pl.pallas_call(
        functools.partial(_merge_kernel, d=d),
        grid=(bsz, t // tm),
        in_specs=[
            pl.BlockSpec((None, tm, d), tok), pl.BlockSpec((None, tm, d), tok),
            pl.BlockSpec((None, tm, d), tok), pl.BlockSpec((None, tm, d), tok),
            pl.BlockSpec((None, tm, 2 * d), tok), pl.BlockSpec((None, tm, d), tok),
            pl.BlockSpec(mod.shape, const),
            pl.BlockSpec((d, d), const), pl.BlockSpec((d, d), const), pl.BlockSpec((d, d), const),
        ],
        out_specs=pl.BlockSpec((None, tm, d), tok),
        out_shape=jax.ShapeDtypeStruct((bsz, t, d), F32),
        compiler_params=pltpu.CompilerParams(
            dimension_semantics=("arbitrary", "arbitrary"), vmem_limit_bytes=VMEM_LIMIT),
        name="merge",
    )(oa, oh, za, zh, mg, x, mod, wba, wbh, wo)


def _rope_tables(t, lc):
    rows = t // GRID_W
    row = jnp.repeat(jnp.arange(rows, dtype=F32), GRID_W)
    col = jnp.tile(jnp.arange(GRID_W, dtype=F32), rows)
    half = MAP_W // 2
    inv = ROPE_BASE ** (-jnp.arange(0, half, 2, dtype=F32) / half)
    ar = row[:, None] * inv[None, :]
    ac = col[:, None] * inv[None, :]
    ang = jnp.concatenate([ar, ar, ac, ac], axis=-1)
    ang = jnp.concatenate([ang, ang], axis=-1)
    cos, sin = jnp.cos(ang), jnp.sin(ang)
    first = (jnp.arange(HEAD_W) % (half)) < (half // 2)
    sin_a = jnp.where(first[None, :], -sin, 0.0)
    sin_b = jnp.where(first[None, :], 0.0, sin)
    pad = lambda a, v: jnp.concatenate([jnp.full((lc, HEAD_W), v, F32), a], axis=0)
    return pad(cos, 1.0), pad(sin_a, 0.0), pad(sin_b, 0.0)


def kernel(x, c, ctx, c_ctx, w_mod, b_mod, norm_gain, w_in, q_norm_gain, k_norm_gain, lambda_q1, lambda_k1,
           lambda_q2, lambda_k2, subln_gain, hgrn_lb_fwd, hgrn_lb_bwd, hgrn_norm_gain, w_br_attn, w_br_hgrn,
           w_out):
    bsz, t, d = x.shape
    lc = ctx.shape[1]
    assert w_mod.shape[0] == 1, "single-layer problem"

    mod_rows = ((bsz + 1 + 7) // 8) * 8
    cc = jnp.concatenate([c, c_ctx[None, :], jnp.zeros((mod_rows - bsz - 1, d), F32)], axis=0)
    mod = _mod_call(cc, w_mod[0], b_mod[0][None, :])

    cos_t, sin_a, sin_b = _rope_tables(t, lc)
    q_gain = jnp.tile(q_norm_gain[0], 2)[None, :] * (MAP_W ** -0.5 * LOG2E)
    k_gain = jnp.tile(k_norm_gain[0], 2)[None, :]
    score_bound = (1.03 * MAP_W) * jnp.max(jnp.abs(q_gain)) * jnp.max(jnp.abs(k_gain))
    bounded = (score_bound <= MAX_SOFTMAX_SHIFT).astype(jnp.int32).reshape(1)
    lane = np.arange(256)
    gsum = jnp.asarray((lane[:, None] // MAP_W) == (lane[None, :] // MAP_W), BF16)
    k_a, v_a, v_h, g_f, g_b, q_a, q_h, z_a, z_h, m_g = _proj_call(
        x, ctx, mod, norm_gain, w_in[0].astype(BF16), q_gain, k_gain, cos_t, sin_a, sin_b,
        hgrn_lb_fwd, hgrn_lb_bwd, gsum)

    o_a = _attn_call(bounded, q_a, k_a, v_a, lambda_q1, lambda_k1, lambda_q2, lambda_k2, subln_gain)
    lb_min = jnp.minimum(jnp.min(jax.nn.softmax(hgrn_lb_fwd.astype(F32), axis=0)[0]),
                         jnp.min(jax.nn.softmax(hgrn_lb_bwd.astype(F32), axis=0)[0]))
    mid_ok = (-jnp.log2(lb_min) * (MID_BLOCK // 2) <= MID_MAX_EXPONENT).astype(jnp.int32).reshape(1)
    o_h = _hgrn_call(mid_ok, q_h, v_h, g_f, g_b, hgrn_norm_gain, lc)
    return _merge_call(o_a, o_h, z_a, z_h, m_g, x, mod,
                       w_br_attn[0].astype(BF16), w_br_hgrn[0].astype(BF16), w_out[0].astype(BF16))
```

```python
import functools
import math

import jax
import jax.numpy as jnp
import numpy as np
from jax import lax
from jax.experimental import pallas as pl
from jax.experimental.pallas import tpu as pltpu

F32 = jnp.float32
BF16 = jnp.bfloat16

HEADS = 8
HEAD_W = 128
MAP_W = 64
GRID_W = 64
ROPE_BASE = 10000.0
EPS = 1e-6
LOG2E = 1.4426950408889634
MAX_SOFTMAX_SHIFT = 50.0
LAM_INIT = 0.8 - 0.6 * math.exp(-0.3 * 0)

PROJ_ROWS = 256
PROJ_COLS = 1024
ATTN_ROWS = 256
KEY_CHUNK = 1024
MERGE_ROWS = 512
CHUNK = 256
LAT_UNROLL = 4
MID_BLOCK = 32
MID_MAX_EXPONENT = 100.0
VMEM_LIMIT = 56 * 1024 * 1024

G_ATTN_K, G_ATTN_V, G_HGRN_I, G_F_FWD, G_F_BWD, G_ATTN_Q, G_HGRN_Q, G_ATTN_Z, G_HGRN_Z, G_MERGE = range(10)

_NT = (((1,), (1,)), ((), ()))


def _sigmoid(a):
    return 1.0 / (1.0 + jnp.exp(-a))


def _mod_kernel(c_ref, w_ref, b_ref, o_ref):
    a = c_ref[...]
    s = a * _sigmoid(a)
    o_ref[...] = jnp.dot(s, w_ref[...], precision=lax.Precision.HIGHEST,
                         preferred_element_type=F32) + b_ref[...]


def _mod_call(cc, w_mod, b_mod):
    rows, d = cc.shape
    n = w_mod.shape[1]
    bn = 1024
    return pl.pallas_call(
        _mod_kernel,
        grid=(n // bn,),
        in_specs=[pl.BlockSpec((rows, d), lambda j: (0, 0)),
                  pl.BlockSpec((d, bn), lambda j: (0, j)),
                  pl.BlockSpec((1, bn), lambda j: (0, j))],
        out_specs=pl.BlockSpec((rows, bn), lambda j: (0, j)),
        out_shape=jax.ShapeDtypeStruct((rows, n), F32),
        name="mod",
    )(cc, w_mod, b_mod)


def _proj_kernel(x_ref, ctx_ref, mod_ref, ng_ref, w_ref, qg_ref, kg_ref, cos_ref, sa_ref, sb_ref,
                 lbf_ref, lbb_ref, gsum_ref,
                 k_out, v_out, hv_out, gf_out, gb_out, q_out, hq_out, za_out, zh_out, mg_out,
                 *, n_ctx_tiles, ctx_row, d):
    b = pl.program_id(0)
    i = pl.program_id(1)
    is_ctx = i < n_ctx_tiles

    u = jnp.where(is_ctx, ctx_ref[...], x_ref[...])
    row = jnp.where(is_ctx, ctx_row, b)
    shift = mod_ref[pl.ds(row, 1), 0:d]
    scale = mod_ref[pl.ds(row, 1), d:2 * d]
    ms = jnp.mean(u * u, axis=-1, keepdims=True)
    y = u * lax.rsqrt(ms + EPS) * ng_ref[...]
    hb = (y * (1.0 + scale) + shift).astype(BF16)

    def qk_norm_rope(p, gain_ref):
        p2 = (p * p).astype(BF16)
        outs = []
        for cch in range(p.shape[1] // 256):
            ss = jnp.dot(p2[:, cch * 256:(cch + 1) * 256], gsum_ref[...], preferred_element_type=F32)
            inv = lax.rsqrt(ss * (1.0 / MAP_W) + EPS)
            for hh in range(2):
                lo = cch * 256 + hh * HEAD_W
                un = p[:, lo:lo + HEAD_W] * inv[:, hh * HEAD_W:(hh + 1) * HEAD_W] * gain_ref[...]
                outs.append(un * cos_ref[...] + pltpu.roll(un, HEAD_W - 16, 1) * sa_ref[...]
                            + pltpu.roll(un, 16, 1) * sb_ref[...])
        return jnp.concatenate(outs, axis=1)

    def lower_bound(ref, lo):
        p = ref[:, lo:lo + PROJ_COLS]
        e = jnp.exp(p - jnp.max(p, axis=0, keepdims=True))
        return e[0:1] / jnp.sum(e, axis=0, keepdims=True)

    def log_forget(a, lb):
        return jnp.log(lb + (1.0 - lb) * _sigmoid(a)) * LOG2E

    silu = lambda a, lo: a * _sigmoid(a)
    common = [
        (G_ATTN_K, k_out, lambda a, lo: qk_norm_rope(a, kg_ref)),
        (G_ATTN_V, v_out, lambda a, lo: a),
        (G_HGRN_I, hv_out, lambda a, lo: a),
        (G_F_FWD, gf_out, lambda a, lo: log_forget(a, lower_bound(lbf_ref, lo))),
        (G_F_BWD, gb_out, lambda a, lo: log_forget(a, lower_bound(lbb_ref, lo))),
    ]
    latent_only = [
        (G_ATTN_Q, q_out, lambda a, lo: qk_norm_rope(a, qg_ref)),
        (G_HGRN_Q, hq_out, silu),
        (G_ATTN_Z, za_out, silu),
        (G_HGRN_Z, zh_out, silu),
        (G_MERGE, mg_out, lambda a, lo: _sigmoid(a)),
    ]

    def run_groups(groups):
        for g, out_ref, epilogue in groups:
            for lo in range(0, out_ref.shape[1], PROJ_COLS):
                col = g * 1024 + lo
                a = jnp.dot(hb, w_ref[:, col:col + PROJ_COLS], preferred_element_type=F32)
                out_ref[:, lo:lo + PROJ_COLS] = epilogue(a, lo).astype(out_ref.dtype)

    run_groups(common)

    @pl.when(jnp.logical_not(is_ctx))
    def _():
        run_groups(latent_only)


def _proj_call(x, ctx, mod, norm_gain, w_bf, q_gain, k_gain, cos_t, sin_a, sin_b, lb_f, lb_b, gsum):
    bsz, t, d = x.shape
    lc = ctx.shape[1]
    tm = PROJ_ROWS
    assert lc % tm == 0 and t % tm == 0
    nct = lc // tm
    s = lc + t
    n_tiles = s // tm
    lat = lambda b, i: (b, jnp.maximum(i - nct, 0), 0)
    cat = lambda b, i: (b, i, 0)
    const2 = lambda b, i: (0, 0)
    kern = functools.partial(_proj_kernel, n_ctx_tiles=nct, ctx_row=bsz, d=d)
    bf = lambda n, w: jax.ShapeDtypeStruct((bsz, n, w), BF16)
    return pl.pallas_call(
        kern,
        grid=(bsz, n_tiles),
        in_specs=[
            pl.BlockSpec((None, tm, d), lat),
            pl.BlockSpec((None, tm, d), lambda b, i: (b, jnp.minimum(i, nct - 1), 0)),
            pl.BlockSpec(mod.shape, const2),
            pl.BlockSpec((1, d), const2),
            pl.BlockSpec(w_bf.shape, const2, pipeline_mode=pl.Buffered(1)),
            pl.BlockSpec((1, HEAD_W), const2),
            pl.BlockSpec((1, HEAD_W), const2),
            pl.BlockSpec((tm, HEAD_W), lambda b, i: (i, 0)),
            pl.BlockSpec((tm, HEAD_W), lambda b, i: (i, 0)),
            pl.BlockSpec((tm, HEAD_W), lambda b, i: (i, 0)),
            pl.BlockSpec(lb_f.shape, const2),
            pl.BlockSpec(lb_b.shape, const2),
            pl.BlockSpec(gsum.shape, const2),
        ],
        out_specs=[
            pl.BlockSpec((None, tm, d), cat),
            pl.BlockSpec((None, tm, d), cat),
            pl.BlockSpec((None, tm, d), cat),
            pl.BlockSpec((None, tm, d), cat),
            pl.BlockSpec((None, tm, d), cat),
            pl.BlockSpec((None, tm, d), lat),
            pl.BlockSpec((None, tm, d), lat),
            pl.BlockSpec((None, tm, d), lat),
            pl.BlockSpec((None, tm, d), lat),
            pl.BlockSpec((None, tm, 2 * d), lat),
        ],
        out_shape=[bf(s, d), bf(s, d), bf(s, d),
                   jax.ShapeDtypeStruct((bsz, s, d), F32), jax.ShapeDtypeStruct((bsz, s, d), F32),
                   bf(t, d), bf(t, d), bf(t, d), bf(t, d), bf(t, 2 * d)],
        compiler_params=pltpu.CompilerParams(
            dimension_semantics=("arbitrary", "arbitrary"), vmem_limit_bytes=VMEM_LIMIT),
        name="proj",
    )(x, ctx, mod, norm_gain, w_bf, q_gain, k_gain, cos_t, sin_a, sin_b, lb_f, lb_b, gsum)


def _attn_kernel(bounded_ref, q_ref, k_ref, v_ref, lq1_ref, lk1_ref, lq2_ref, lk2_ref, sg_ref, o_ref,
                 e_scr, c_scr, acc_scr, vt_scr, *, tq):
    s = k_ref.shape[0]
    n_tiles = q_ref.shape[0] // tq
    chunks = [(st, min(KEY_CHUNK, s - st)) for st in range(0, s, KEY_CHUNK)]
    vt_scr[...] = v_ref[...].astype(F32).T.astype(BF16)
    lam = (jnp.exp(jnp.sum(lq1_ref[...] * lk1_ref[...], axis=-1, keepdims=True))
           - jnp.exp(jnp.sum(lq2_ref[...] * lk2_ref[...], axis=-1, keepdims=True)) + LAM_INIT)

    def map_queries(i):
        q = q_ref[pl.ds(pl.multiple_of(i * tq, tq), tq), :]
        lane = lax.broadcasted_iota(jnp.int32, q.shape, 1)
        zero = jnp.zeros_like(q)
        return jnp.where(lane < MAP_W, q, zero), jnp.where(lane >= MAP_W, q, zero)

    def scores(kc, qm):
        return lax.dot_general(kc, qm, _NT, preferred_element_type=F32)

    def stage_a_bounded(i, slot):
        qs = map_queries(i)
        sums = [jnp.zeros((1, tq), F32), jnp.zeros((1, tq), F32)]
        for st, sz in chunks:
            kc = k_ref[st:st + sz, :]
            for m in range(2):
                e = jnp.exp2(scores(kc, qs[m]))
                sums[m] = sums[m] + jnp.sum(e, axis=0, keepdims=True)
                e_scr[slot, m, st:st + sz, :] = e.astype(BF16)
            yield
        c_scr[slot, 0:1, :] = 1.0 / sums[0]
        c_scr[slot, 1:2, :] = lam / sums[1]

    def run(*stages):
        live = list(stages)
        while live:
            live = [g for g in live if next(g, StopIteration) is not StopIteration]

    def stage_a_general(i, slot):
        qs = map_queries(i)
        k = k_ref[...]
        for m in range(2):
            sm = scores(k, qs[m])
            e = jnp.exp2(sm - jnp.max(sm, axis=0, keepdims=True))
            c_scr[slot, m:m + 1, :] = (lam if m else 1.0) / jnp.sum(e, axis=0, keepdims=True)
            e_scr[slot, m] = e.astype(BF16)

    def stage_b(i, slot):
        c1 = c_scr[slot, 0:1, :].astype(BF16)
        c2 = c_scr[slot, 1:2, :].astype(BF16)
        acc = jnp.zeros((vt_scr.shape[0], tq), F32)
        for st, sz in chunks:
            w = e_scr[slot, 0, st:st + sz, :] * c1 - e_scr[slot, 1, st:st + sz, :] * c2
            acc = acc + jnp.dot(vt_scr[:, st:st + sz], w, preferred_element_type=F32)
            yield
        acc_scr[slot] = acc

    def stage_c(i, slot):
        o = acc_scr[slot].T
        ms = jnp.mean(o * o, axis=-1, keepdims=True)
        o = o * lax.rsqrt(ms + EPS) * sg_ref[...] * (1.0 - LAM_INIT)
        o_ref[pl.ds(pl.multiple_of(i * tq, tq), tq), :] = o.astype(o_ref.dtype)
        yield

    @pl.when(bounded_ref[0] != 0)
    def _():
        run(stage_a_bounded(0, 0))
        run(stage_a_bounded(1, 1), stage_b(0, 0))

        def pair(j, _):
            i = 2 * j
            run(stage_c(i, 0), stage_a_bounded(i + 2, 0), stage_b(i + 1, 1))
            run(stage_c(i + 1, 1), stage_a_bounded(i + 3, 1), stage_b(i + 2, 0))
            return 0

        lax.fori_loop(0, n_tiles // 2 - 1, pair, 0)
        run(stage_c(n_tiles - 2, 0), stage_b(n_tiles - 1, 1))
        run(stage_c(n_tiles - 1, 1))

    @pl.when(bounded_ref[0] == 0)
    def _():
        def tile(i, _):
            stage_a_general(i, 0)
            run(stage_b(i, 0))
            run(stage_c(i, 0))
            return 0

        lax.fori_loop(0, n_tiles, tile, 0)


def _attn_call(bounded, q, k, v, lq1, lk1, lq2, lk2, subln_gain):
    bsz, t, d = q.shape
    s = k.shape[1]
    tq = ATTN_ROWS
    assert t % (2 * tq) == 0
    const = lambda b, h: (0, 0)
    head = lambda b, h: (b, 0, h)
    return pl.pallas_call(
        functools.partial(_attn_kernel, tq=tq),
        grid=(bsz, HEADS),
        in_specs=[
            pl.BlockSpec(memory_space=pltpu.SMEM),
            pl.BlockSpec((None, t, HEAD_W), head),
            pl.BlockSpec((None, s, HEAD_W), head),
            pl.BlockSpec((None, s, HEAD_W), head),
            pl.BlockSpec((1, MAP_W), const), pl.BlockSpec((1, MAP_W), const),
            pl.BlockSpec((1, MAP_W), const), pl.BlockSpec((1, MAP_W), const),
            pl.BlockSpec((1, HEAD_W), const),
        ],
        out_specs=pl.BlockSpec((None, t, HEAD_W), head),
        out_shape=jax.ShapeDtypeStruct((bsz, t, d), BF16),
        scratch_shapes=[pltpu.VMEM((2, 2, s, tq), BF16), pltpu.VMEM((2, 8, tq), F32),
                        pltpu.VMEM((2, HEAD_W, tq), F32), pltpu.VMEM((HEAD_W, s), BF16)],
        compiler_params=pltpu.CompilerParams(
            dimension_semantics=("arbitrary", "arbitrary"), vmem_limit_bytes=VMEM_LIMIT),
        name="attn",
    )(bounded, q, k, v, lq1, lk1, lq2, lk2, subln_gain)


def _prefix_matrix(c, rev):
    t = np.arange(c)
    m = (t[None, :] >= t[:, None]) if rev else (t[None, :] <= t[:, None])
    return jnp.asarray(m.astype(np.float32), BF16)


def _split2(g):
    hi = g.astype(BF16)
    mid = (g - hi.astype(F32)).astype(BF16)
    return jnp.concatenate([hi, mid], axis=1)


def _cum_decay(l_ref, g, b_scr, rev):
    c, w = g.shape
    e = jnp.dot(l_ref[...], _split2(g), preferred_element_type=F32)
    b = e[:, 0:w] + e[:, w:2 * w]
    b_scr[...] = b
    tot = b_scr[0:1, :] if rev else b_scr[c - 1:c, :]
    return b, tot


def _state_update(st, v, kk, e_to_end, e_total):
    kd = (kk * e_to_end).astype(BF16)
    vt = v.astype(F32).T.astype(BF16)
    return st * e_total + jnp.dot(vt, kd, preferred_element_type=F32)


def _ctx_chunk(st, g, v, l_ref, b_scr, rev):
    b, tot = _cum_decay(l_ref, g, b_scr, rev)
    kk = 1.0 - jnp.exp2(g)
    return _state_update(st, v, kk, jnp.exp2(tot - b), jnp.exp2(tot))


def _level_operand(m, qf, kk, eg, b, b_scr, rev):
    c, w = b.shape
    ref = m if rev else m - 1
    bcast = lambda r, n: jnp.broadcast_to(b_scr[r:r + 1, :], (n, w))
    if m >= 8:
        pieces = []
        for j in range(c // (2 * m)):
            lo = slice(j * 2 * m, j * 2 * m + m)
            hi = slice(j * 2 * m + m, (j + 1) * 2 * m)
            r = bcast(j * 2 * m + ref, m)
            if rev:
                pieces += [qf[lo] * jnp.exp2(b[lo] - r), kk[hi] * jnp.exp2(r - b[hi])]
            else:
                pieces += [kk[lo] * jnp.exp2(r - b[lo]), qf[hi] * jnp.exp2(b[hi] - r)]
        return jnp.concatenate(pieces, axis=0).astype(BF16)
    row = lax.broadcasted_iota(jnp.int32, (c, w), 0)
    upper = (row & m) != 0
    q_rows = jnp.logical_not(upper) if rev else upper
    if m == 1:
        f = jnp.where(q_rows, eg, 1.0)
    else:
        if m == 4:
            r = jnp.concatenate([bcast(8 * j + ref, 8) for j in range(c // 8)], axis=0)
        else:
            sub = lax.broadcasted_iota(jnp.int32, (8, w), 0)
            r = jnp.concatenate([jnp.where(sub < 4, bcast(8 * j + ref, 8), bcast(8 * j + 4 + ref, 8))
                                 for j in range(c // 8)], axis=0)
        f = jnp.exp2(-jnp.abs(b - r))
    return (jnp.where(q_rows, qf, kk) * f).astype(BF16)


def _mid_operands(qf, kk, b, b_scr, rev):
    c, w = b.shape
    mid = MID_BLOCK // 2
    r = jnp.concatenate([jnp.broadcast_to(b_scr[j * MID_BLOCK + mid:j * MID_BLOCK + mid + 1, :], (MID_BLOCK, w))
                         for j in range(c // MID_BLOCK)], axis=0)
    d = b - r
    return (qf * jnp.exp2(d)).astype(BF16), (kk * jnp.exp2(-d)).astype(BF16)


def _chunk_decays(ch):
    b, tot = _cum_decay(ch["l_ref"], ch["g"], ch["b_scr"], ch["rev"])
    eg = jnp.exp2(ch["g"])
    ch.update(b=b, tot=tot, eg=eg, kk=1.0 - eg, qf=ch["q"].astype(F32))


def _chunk_state(st, ch):
    b, tot, kk, qf, v = ch["b"], ch["tot"], ch["kk"], ch["qf"], ch["v"]
    st_new = _state_update(st, v, kk, jnp.exp2(tot - b), jnp.exp2(tot))
    qd = (qf * jnp.exp2(b)).astype(BF16)
    o = lax.dot_general(qd, st.astype(BF16), _NT, preferred_element_type=F32)
    ch["o"] = o + jnp.sum(qf * kk, axis=-1, keepdims=True) * v.astype(F32)
    return st_new


def _chunk_level(ch, m, differ):
    qf, kk, eg, b, b_scr, rev = ch["qf"], ch["kk"], ch["eg"], ch["b"], ch["b_scr"], ch["rev"]
    c = b.shape[0]
    h = c // 2
    if m == "mid":
        xq, xk = _mid_operands(qf, kk, b, b_scr, rev)
        block = MID_BLOCK
    else:
        xq = xk = _level_operand(m, qf, kk, eg, b, b_scr, rev)
        block = 2 * m
    p = lax.dot_general(xq, xk, _NT, preferred_element_type=F32)
    if block == c:
        ch["a_off"] = (p[0:h, h:c] if rev else p[h:c, 0:h]).astype(BF16)
    elif block == h:
        ch["a0"], ch["a1"] = p[0:h, 0:h], p[h:c, h:c]
    else:
        same_block = differ < block
        ch["a0"] = jnp.where(same_block, p[0:h, 0:h], ch["a0"])
        ch["a1"] = jnp.where(same_block, p[h:c, h:c], ch["a1"])


def _chunk_output(ch, valid):
    v, rev = ch["v"], ch["rev"]
    c = v.shape[0]
    h = c // 2
    a0 = jnp.where(valid, ch["a0"], 0.0).astype(BF16)
    a1 = jnp.where(valid, ch["a1"], 0.0).astype(BF16)
    if rev:
        o_lo = jnp.dot(jnp.concatenate([a0, ch["a_off"]], axis=1), v, preferred_element_type=F32)
        o_hi = jnp.dot(a1, v[h:c], preferred_element_type=F32)
    else:
        o_lo = jnp.dot(a0, v[0:h], preferred_element_type=F32)
        o_hi = jnp.dot(jnp.concatenate([ch["a_off"], a1], axis=1), v, preferred_element_type=F32)
    return ch["o"] + jnp.concatenate([o_lo, o_hi], axis=0)


def _lat_group(chunks, st_f, st_b, mid_ok):
    c = chunks[0]["g"].shape[0]
    h = c // 2
    for ch in chunks:
        _chunk_decays(ch)
    for ch in chunks:
        if ch["rev"]:
            st_b = _chunk_state(st_b, ch)
        else:
            st_f = _chunk_state(st_f, ch)
    ti = lax.broadcasted_iota(jnp.int32, (h, h), 0)
    si = lax.broadcasted_iota(jnp.int32, (h, h), 1)
    differ = ti ^ si
    first_lvl = int(math.log2(MID_BLOCK)) if mid_ok else 0
    levels = [1 << lvl for lvl in reversed(range(first_lvl, int(math.log2(c))))] + (["mid"] if mid_ok else [])
    for m in levels:
        for ch in chunks:
            _chunk_level(ch, m, differ)
    outs = [_chunk_output(ch, (ti < si) if ch["rev"] else (ti > si)) for ch in chunks]
    return st_f, st_b, outs


def _hgrn_kernel(mid_ref, q_ref, v_ref, gf_ref, gb_ref, lf_ref, lb_ref, hg_ref, o_ref, of_scr, ob_scr, bf_scr,
                 bb_scr, *, lc, c):
    t = q_ref.shape[0]
    dk = gf_ref.shape[1]
    dv = v_ref.shape[1]
    n_ctx = lc // c
    n_lat = t // c
    st0 = jnp.zeros((dv, dk), F32)

    def ctx(j, carry):
        st_f, st_b = carry
        rf = pl.multiple_of(j * c, c)
        rb = pl.multiple_of((n_ctx - 1 - j) * c, c)
        st_f = _ctx_chunk(st_f, gf_ref[pl.ds(rf, c), :], v_ref[pl.ds(rf, c), :], lf_ref, bf_scr.at[0], False)
        st_b = _ctx_chunk(st_b, gb_ref[pl.ds(rb, c), :], v_ref[pl.ds(rb, c), :], lb_ref, bb_scr.at[0], True)
        return st_f, st_b

    carry = lax.fori_loop(0, n_ctx, ctx, (st0, st0))

    def latents(mid_ok):
        def lat(jj, carry):
            st_f, st_b = carry
            chunks = []
            for u in range(LAT_UNROLL):
                j = jj * LAT_UNROLL + u
                rf = pl.multiple_of(j * c, c)
                rb = pl.multiple_of((n_lat - 1 - j) * c, c)
                sf = pl.multiple_of(lc + j * c, c)
                sb = pl.multiple_of(lc + (n_lat - 1 - j) * c, c)
                chunks.append(dict(g=gf_ref[pl.ds(sf, c), :], q=q_ref[pl.ds(rf, c), :], v=v_ref[pl.ds(sf, c), :],
                                   l_ref=lf_ref, b_scr=bf_scr.at[u], rev=False, out=of_scr, row=rf))
                chunks.append(dict(g=gb_ref[pl.ds(sb, c), :], q=q_ref[pl.ds(rb, c), :], v=v_ref[pl.ds(sb, c), :],
                                   l_ref=lb_ref, b_scr=bb_scr.at[u], rev=True, out=ob_scr, row=rb))
            st_f, st_b, outs = _lat_group(chunks, st_f, st_b, mid_ok)
            for ch, o in zip(chunks, outs):
                ch["out"][pl.ds(ch["row"], c), :] = o
            return st_f, st_b

        lax.fori_loop(0, n_lat // LAT_UNROLL, lat, carry)

    @pl.when(mid_ref[0] != 0)
    def _():
        latents(True)

    @pl.when(mid_ref[0] == 0)
    def _():
        latents(False)

    o = of_scr[...] + ob_scr[...]
    ms = jnp.mean(o * o, axis=-1, keepdims=True)
    o_ref[...] = (o * lax.rsqrt(ms + EPS) * hg_ref[...]).astype(o_ref.dtype)


def _hgrn_call(mid_ok, hq, hv, gf, gb, hgrn_gain, lc):
    bsz, t, d = hq.shape
    s = hv.shape[1]
    c = CHUNK
    assert lc % c == 0 and t % c == 0
    lf = _prefix_matrix(c, False)
    lb = _prefix_matrix(c, True)
    const = lambda b, h: (0, 0)
    kern = functools.partial(_hgrn_kernel, lc=lc, c=c)
    return pl.pallas_call(
        kern,
        grid=(bsz, HEADS),
        in_specs=[
            pl.BlockSpec(memory_space=pltpu.SMEM),
            pl.BlockSpec((None, t, HEAD_W), lambda b, h: (b, 0, h)),
            pl.BlockSpec((None, s, HEAD_W), lambda b, h: (b, 0, h)),
            pl.BlockSpec((None, s, HEAD_W), lambda b, h: (b, 0, h)),
            pl.BlockSpec((None, s, HEAD_W), lambda b, h: (b, 0, h)),
            pl.BlockSpec(lf.shape, const), pl.BlockSpec(lb.shape, const),
            pl.BlockSpec((1, HEAD_W), const),
        ],
        out_specs=pl.BlockSpec((None, t, HEAD_W), lambda b, h: (b, 0, h)),
        out_shape=jax.ShapeDtypeStruct((bsz, t, d), BF16),
        scratch_shapes=[pltpu.VMEM((t, HEAD_W), F32), pltpu.VMEM((t, HEAD_W), F32),
                        pltpu.VMEM((LAT_UNROLL, c, HEAD_W), F32), pltpu.VMEM((LAT_UNROLL, c, HEAD_W), F32)],
        compiler_params=pltpu.CompilerParams(
            dimension_semantics=("arbitrary", "arbitrary"), vmem_limit_bytes=VMEM_LIMIT),
        name="hgrn",
    )(mid_ok, hq, hv, gf, gb, lf, lb, hgrn_gain)


def _merge_kernel(oa_ref, oh_ref, za_ref, zh_ref, mg_ref, x_ref, mod_ref, wba_ref, wbh_ref, wo_ref, out_ref, *, d):
    b = pl.program_id(0)
    ya = oa_ref[...] * za_ref[...]
    yh = oh_ref[...] * zh_ref[...]
    ta = jnp.dot(ya, wba_ref[...], preferred_element_type=F32)
    th = jnp.dot(yh, wbh_ref[...], preferred_element_type=F32)
    mix = mg_ref[:, 0:d].astype(F32) * ta + mg_ref[:, d:2 * d].astype(F32) * th
    y = jnp.dot(mix.astype(BF16), wo_ref[...], preferred_element_type=F32)
    gate = mod_ref[pl.ds(b, 1), 2 * d:3 * d]
    out_ref[...] = x_ref[...] + gate * y


def _merge_call(oa, oh, za, zh, mg, x, mod, wba, wbh, wo):
    bsz, t, d = x.shape
    tm = MERGE_ROWS
    tok = lambda b, i: (b, i, 0)
    const = lambda b, i: (0, 0)
    return pl.pallas_call(
        functools.partial(_merge_kernel, d=d),
        grid=(bsz, t // tm),
        in_specs=[
            pl.BlockSpec((None, tm, d), tok), pl.BlockSpec((None, tm, d), tok),
            pl.BlockSpec((None, tm, d), tok), pl.BlockSpec((None, tm, d), tok),
            pl.BlockSpec((None, tm, 2 * d), tok), pl.BlockSpec((None, tm, d), tok),
            pl.BlockSpec(mod.shape, const),
            pl.BlockSpec((d, d), const), pl.BlockSpec((d, d), const), pl.BlockSpec((d, d), const),
        ],
        out_specs=pl.BlockSpec((None, tm, d), tok),
        out_shape=jax.ShapeDtypeStruct((bsz, t, d), F32),
        compiler_params=pltpu.CompilerParams(
            dimension_semantics=("arbitrary", "arbitrary"), vmem_limit_bytes=VMEM_LIMIT),
        name="merge",
    )(oa, oh, za, zh, mg, x, mod, wba, wbh, wo)


def _rope_tables(t, lc):
    rows = t // GRID_W
    row = jnp.repeat(jnp.arange(rows, dtype=F32), GRID_W)
    col = jnp.tile(jnp.arange(GRID_W, dtype=F32), rows)
    half = MAP_W // 2
    inv = ROPE_BASE ** (-jnp.arange(0, half, 2, dtype=F32) / half)
    ar = row[:, None] * inv[None, :]
    ac = col[:, None] * inv[None, :]
    ang = jnp.concatenate([ar, ar, ac, ac], axis=-1)
    ang = jnp.concatenate([ang, ang], axis=-1)
    cos, sin = jnp.cos(ang), jnp.sin(ang)
    first = (jnp.arange(HEAD_W) % (half)) < (half // 2)
    sin_a = jnp.where(first[None, :], -sin, 0.0)
    sin_b = jnp.where(first[None, :], 0.0, sin)
    pad = lambda a, v: jnp.concatenate([jnp.full((lc, HEAD_W), v, F32), a], axis=0)
    return pad(cos, 1.0), pad(sin_a, 0.0), pad(sin_b, 0.0)


def kernel(x, c, ctx, c_ctx, w_mod, b_mod, norm_gain, w_in, q_norm_gain, k_norm_gain, lambda_q1, lambda_k1,
           lambda_q2, lambda_k2, subln_gain, hgrn_lb_fwd, hgrn_lb_bwd, hgrn_norm_gain, w_br_attn, w_br_hgrn,
           w_out):
    bsz, t, d = x.shape
    lc = ctx.shape[1]
    assert w_mod.shape[0] == 1, "single-layer problem"

    mod_rows = ((bsz + 1 + 7) // 8) * 8
    cc = jnp.concatenate([c, c_ctx[None, :], jnp.zeros((mod_rows - bsz - 1, d), F32)], axis=0)
    mod = _mod_call(cc, w_mod[0], b_mod[0][None, :])

    cos_t, sin_a, sin_b = _rope_tables(t, lc)
    q_gain = jnp.tile(q_norm_gain[0], 2)[None, :] * (MAP_W ** -0.5 * LOG2E)
    k_gain = jnp.tile(k_norm_gain[0], 2)[None, :]
    score_bound = (1.03 * MAP_W) * jnp.max(jnp.abs(q_gain)) * jnp.max(jnp.abs(k_gain))
    bounded = (score_bound <= MAX_SOFTMAX_SHIFT).astype(jnp.int32).reshape(1)
    lane = np.arange(256)
    gsum = jnp.asarray((lane[:, None] // MAP_W) == (lane[None, :] // MAP_W), BF16)
    k_a, v_a, v_h, g_f, g_b, q_a, q_h, z_a, z_h, m_g = _proj_call(
        x, ctx, mod, norm_gain, w_in[0].astype(BF16), q_gain, k_gain, cos_t, sin_a, sin_b,
        hgrn_lb_fwd, hgrn_lb_bwd, gsum)

    o_a = _attn_call(bounded, q_a, k_a, v_a, lambda_q1, lambda_k1, lambda_q2, lambda_k2, subln_gain)
    lb_min = jnp.minimum(jnp.min(jax.nn.softmax(hgrn_lb_fwd.astype(F32), axis=0)[0]),
                         jnp.min(jax.nn.softmax(hgrn_lb_bwd.astype(F32), axis=0)[0]))
    mid_ok = (-jnp.log2(lb_min) * (MID_BLOCK // 2) <= MID_MAX_EXPONENT).astype(jnp.int32).reshape(1)
    o_h = _hgrn_call(mid_ok, q_h, v_h, g_f, g_b, hgrn_norm_gain, lc)
    return _merge_call(o_a, o_h, z_a, z_h, m_g, x, mod,
                       w_br_attn[0].astype(BF16), w_br_hgrn[0].astype(BF16), w_out[0].astype(BF16))
```

```python
import functools
import math

import jax
import jax.numpy as jnp
import numpy as np
from jax import lax
from jax.experimental import pallas as pl
from jax.experimental.pallas import tpu as pltpu

F32 = jnp.float32
BF16 = jnp.bfloat16

HEADS = 8
HEAD_W = 128
MAP_W = 64
GRID_W = 64
ROPE_BASE = 10000.0
EPS = 1e-6
LOG2E = 1.4426950408889634
MAX_SOFTMAX_SHIFT = 50.0
LAM_INIT = 0.8 - 0.6 * math.exp(-0.3 * 0)

PROJ_ROWS = 256
PROJ_COLS = 1024
ATTN_ROWS = 256
ATTN_HEADS_PER_STEP = 1
KEY_CHUNK = 1024
MERGE_ROWS = 512
CHUNK = 256
LAT_UNROLL = 4
MID_BLOCK = 32
MID_MAX_EXPONENT = 100.0
VMEM_LIMIT = 56 * 1024 * 1024

G_ATTN_K, G_ATTN_V, G_HGRN_I, G_F_FWD, G_F_BWD, G_ATTN_Q, G_HGRN_Q, G_ATTN_Z, G_HGRN_Z, G_MERGE = range(10)

_NT = (((1,), (1,)), ((), ()))


def _sigmoid(a):
    return 1.0 / (1.0 + jnp.exp(-a))


def _mod_kernel(c_ref, w_ref, b_ref, o_ref):
    a = c_ref[...]
    s = a * _sigmoid(a)
    o_ref[...] = jnp.dot(s, w_ref[...], precision=lax.Precision.HIGHEST,
                         preferred_element_type=F32) + b_ref[...]


def _mod_call(cc, w_mod, b_mod):
    rows, d = cc.shape
    n = w_mod.shape[1]
    bn = 1024
    return pl.pallas_call(
        _mod_kernel,
        grid=(n // bn,),
        in_specs=[pl.BlockSpec((rows, d), lambda j: (0, 0)),
                  pl.BlockSpec((d, bn), lambda j: (0, j)),
                  pl.BlockSpec((1, bn), lambda j: (0, j))],
        out_specs=pl.BlockSpec((rows, bn), lambda j: (0, j)),
        out_shape=jax.ShapeDtypeStruct((rows, n), F32),
        name="mod",
    )(cc, w_mod, b_mod)


def _proj_kernel(x_ref, ctx_ref, mod_ref, ng_ref, w_ref, qg_ref, kg_ref, cos_ref, sa_ref, sb_ref,
                 lbf_ref, lbb_ref, gsum_ref,
                 k_out, v_out, hv_out, gf_out, gb_out, q_out, hq_out, za_out, zh_out, mg_out,
                 *, n_ctx_tiles, ctx_row, d):
    b = pl.program_id(0)
    i = pl.program_id(1)
    is_ctx = i < n_ctx_tiles

    u = jnp.where(is_ctx, ctx_ref[...], x_ref[...])
    row = jnp.where(is_ctx, ctx_row, b)
    shift = mod_ref[pl.ds(row, 1), 0:d]
    scale = mod_ref[pl.ds(row, 1), d:2 * d]
    ms = jnp.mean(u * u, axis=-1, keepdims=True)
    y = u * lax.rsqrt(ms + EPS) * ng_ref[...]
    hb = (y * (1.0 + scale) + shift).astype(BF16)

    def qk_norm_rope(p, gain_ref):
        p2 = (p * p).astype(BF16)
        outs = []
        for cch in range(p.shape[1] // 256):
            ss = jnp.dot(p2[:, cch * 256:(cch + 1) * 256], gsum_ref[...], preferred_element_type=F32)
            inv = lax.rsqrt(ss * (1.0 / MAP_W) + EPS)
            for hh in range(2):
                lo = cch * 256 + hh * HEAD_W
                un = p[:, lo:lo + HEAD_W] * inv[:, hh * HEAD_W:(hh + 1) * HEAD_W] * gain_ref[...]
                outs.append(un * cos_ref[...] + pltpu.roll(un, HEAD_W - 16, 1) * sa_ref[...]
                            + pltpu.roll(un, 16, 1) * sb_ref[...])
        return jnp.concatenate(outs, axis=1)

    def lower_bound(ref, lo):
        p = ref[:, lo:lo + PROJ_COLS]
        e = jnp.exp(p - jnp.max(p, axis=0, keepdims=True))
        return e[0:1] / jnp.sum(e, axis=0, keepdims=True)

    def log_forget(a, lb):
        return jnp.log(lb + (1.0 - lb) * _sigmoid(a)) * LOG2E

    silu = lambda a, lo: a * _sigmoid(a)
    common = [
        (G_ATTN_K, k_out, lambda a, lo: qk_norm_rope(a, kg_ref)),
        (G_ATTN_V, v_out, lambda a, lo: a),
        (G_HGRN_I, hv_out, lambda a, lo: a),
        (G_F_FWD, gf_out, lambda a, lo: log_forget(a, lower_bound(lbf_ref, lo))),
        (G_F_BWD, gb_out, lambda a, lo: log_forget(a, lower_bound(lbb_ref, lo))),
    ]
    latent_only = [
        (G_ATTN_Q, q_out, lambda a, lo: qk_norm_rope(a, qg_ref)),
        (G_HGRN_Q, hq_out, silu),
        (G_ATTN_Z, za_out, silu),
        (G_HGRN_Z, zh_out, silu),
        (G_MERGE, mg_out, lambda a, lo: _sigmoid(a)),
    ]

    def run_groups(groups):
        for g, out_ref, epilogue in groups:
            for lo in range(0, out_ref.shape[1], PROJ_COLS):
                col = g * 1024 + lo
                a = jnp.dot(hb, w_ref[:, col:col + PROJ_COLS], preferred_element_type=F32)
                out_ref[:, lo:lo + PROJ_COLS] = epilogue(a, lo).astype(out_ref.dtype)

    run_groups(common)

    @pl.when(jnp.logical_not(is_ctx))
    def _():
        run_groups(latent_only)


def _proj_call(x, ctx, mod, norm_gain, w_bf, q_gain, k_gain, cos_t, sin_a, sin_b, lb_f, lb_b, gsum):
    bsz, t, d = x.shape
    lc = ctx.shape[1]
    tm = PROJ_ROWS
    assert lc % tm == 0 and t % tm == 0
    nct = lc // tm
    s = lc + t
    n_tiles = s // tm
    lat = lambda b, i: (b, jnp.maximum(i - nct, 0), 0)
    cat = lambda b, i: (b, i, 0)
    const2 = lambda b, i: (0, 0)
    kern = functools.partial(_proj_kernel, n_ctx_tiles=nct, ctx_row=bsz, d=d)
    bf = lambda n, w: jax.ShapeDtypeStruct((bsz, n, w), BF16)
    return pl.pallas_call(
        kern,
        grid=(bsz, n_tiles),
        in_specs=[
            pl.BlockSpec((None, tm, d), lat),
            pl.BlockSpec((None, tm, d), lambda b, i: (b, jnp.minimum(i, nct - 1), 0)),
            pl.BlockSpec(mod.shape, const2),
            pl.BlockSpec((1, d), const2),
            pl.BlockSpec(w_bf.shape, const2, pipeline_mode=pl.Buffered(1)),
            pl.BlockSpec((1, HEAD_W), const2),
            pl.BlockSpec((1, HEAD_W), const2),
            pl.BlockSpec((tm, HEAD_W), lambda b, i: (i, 0)),
            pl.BlockSpec((tm, HEAD_W), lambda b, i: (i, 0)),
            pl.BlockSpec((tm, HEAD_W), lambda b, i: (i, 0)),
            pl.BlockSpec(lb_f.shape, const2),
            pl.BlockSpec(lb_b.shape, const2),
            pl.BlockSpec(gsum.shape, const2),
        ],
        out_specs=[
            pl.BlockSpec((None, tm, d), cat),
            pl.BlockSpec((None, tm, d), cat),
            pl.BlockSpec((None, tm, d), cat),
            pl.BlockSpec((None, tm, d), cat),
            pl.BlockSpec((None, tm, d), cat),
            pl.BlockSpec((None, tm, d), lat),
            pl.BlockSpec((None, tm, d), lat),
            pl.BlockSpec((None, tm, d), lat),
            pl.BlockSpec((None, tm, d), lat),
            pl.BlockSpec((None, tm, 2 * d), lat),
        ],
        out_shape=[bf(s, d), bf(s, d), bf(s, d),
                   jax.ShapeDtypeStruct((bsz, s, d), F32), jax.ShapeDtypeStruct((bsz, s, d), F32),
                   bf(t, d), bf(t, d), bf(t, d), bf(t, d), bf(t, 2 * d)],
        compiler_params=pltpu.CompilerParams(
            dimension_semantics=("arbitrary", "arbitrary"), vmem_limit_bytes=VMEM_LIMIT),
        name="proj",
    )(x, ctx, mod, norm_gain, w_bf, q_gain, k_gain, cos_t, sin_a, sin_b, lb_f, lb_b, gsum)


def _attn_kernel(bounded_ref, q_ref, k_ref, v_ref, lq1_ref, lk1_ref, lq2_ref, lk2_ref, sg_ref, o_ref,
                 e_scr, c_scr, acc_scr, vt_scr, *, tq):
    s = k_ref.shape[0]
    n_tiles = q_ref.shape[0] // tq
    n_heads = q_ref.shape[1] // HEAD_W
    chunks = [(st, min(KEY_CHUNK, s - st)) for st in range(0, s, KEY_CHUNK)]
    cols = lambda hd: slice(hd * HEAD_W, (hd + 1) * HEAD_W)
    for hd in range(n_heads):
        vt_scr[hd] = v_ref[:, cols(hd)].astype(F32).T.astype(BF16)
    lam = (jnp.exp(jnp.sum(lq1_ref[...] * lk1_ref[...], axis=-1, keepdims=True))
           - jnp.exp(jnp.sum(lq2_ref[...] * lk2_ref[...], axis=-1, keepdims=True)) + LAM_INIT)

    def map_queries(hd, i):
        q = q_ref[pl.ds(pl.multiple_of(i * tq, tq), tq), cols(hd)]
        lane = lax.broadcasted_iota(jnp.int32, q.shape, 1)
        zero = jnp.zeros_like(q)
        return jnp.where(lane < MAP_W, q, zero), jnp.where(lane >= MAP_W, q, zero)

    def scores(kc, qm):
        return lax.dot_general(kc, qm, _NT, preferred_element_type=F32)

    def stage_a_bounded(hd, i, slot):
        qs = map_queries(hd, i)
        sums = [jnp.zeros((1, tq), F32), jnp.zeros((1, tq), F32)]
        for st, sz in chunks:
            kc = k_ref[st:st + sz, cols(hd)]
            for m in range(2):
                e = jnp.exp2(scores(kc, qs[m]))
                sums[m] = sums[m] + jnp.sum(e, axis=0, keepdims=True)
                e_scr[slot, m, st:st + sz, :] = e.astype(BF16)
            yield
        c_scr[slot, 0:1, :] = 1.0 / sums[0]
        c_scr[slot, 1:2, :] = lam / sums[1]

    def run(*stages):
        live = list(stages)
        while live:
            live = [g for g in live if next(g, StopIteration) is not StopIteration]

    def stage_a_general(hd, i, slot):
        qs = map_queries(hd, i)
        k = k_ref[:, cols(hd)]
        for m in range(2):
            sm = scores(k, qs[m])
            e = jnp.exp2(sm - jnp.max(sm, axis=0, keepdims=True))
            c_scr[slot, m:m + 1, :] = (lam if m else 1.0) / jnp.sum(e, axis=0, keepdims=True)
            e_scr[slot, m] = e.astype(BF16)

    def stage_b(hd, i, slot):
        c1 = c_scr[slot, 0:1, :].astype(BF16)
        c2 = c_scr[slot, 1:2, :].astype(BF16)
        acc = jnp.zeros((HEAD_W, tq), F32)
        for st, sz in chunks:
            w = e_scr[slot, 0, st:st + sz, :] * c1 - e_scr[slot, 1, st:st + sz, :] * c2
            acc = acc + jnp.dot(vt_scr[hd, :, st:st + sz], w, preferred_element_type=F32)
            yield
        acc_scr[slot] = acc

    def stage_c(hd, i, slot):
        o = acc_scr[slot].T
        ms = jnp.mean(o * o, axis=-1, keepdims=True)
        o = o * lax.rsqrt(ms + EPS) * sg_ref[...] * (1.0 - LAM_INIT)
        o_ref[pl.ds(pl.multiple_of(i * tq, tq), tq), cols(hd)] = o.astype(o_ref.dtype)
        yield

    @pl.when(bounded_ref[0] != 0)
    def _():
        last = n_tiles - 1
        run(stage_a_bounded(0, 0, 0))
        run(stage_a_bounded(0, 1, 1), stage_b(0, 0, 0))
        for hd in range(n_heads):
            def pair(j, _, hd=hd):
                i = 2 * j
                run(stage_c(hd, i, 0), stage_a_bounded(hd, i + 2, 0), stage_b(hd, i + 1, 1))
                run(stage_c(hd, i + 1, 1), stage_a_bounded(hd, i + 3, 1), stage_b(hd, i + 2, 0))
                return 0

            lax.fori_loop(0, n_tiles // 2 - 1, pair, 0)
            if hd + 1 < n_heads:
                run(stage_c(hd, last - 1, 0), stage_a_bounded(hd + 1, 0, 0), stage_b(hd, last, 1))
                run(stage_c(hd, last, 1), stage_a_bounded(hd + 1, 1, 1), stage_b(hd + 1, 0, 0))
            else:
                run(stage_c(hd, last - 1, 0), stage_b(hd, last, 1))
                run(stage_c(hd, last, 1))

    @pl.when(bounded_ref[0] == 0)
    def _():
        for hd in range(n_heads):
            def tile(i, _, hd=hd):
                stage_a_general(hd, i, 0)
                run(stage_b(hd, i, 0))
                run(stage_c(hd, i, 0))
                return 0

            lax.fori_loop(0, n_tiles, tile, 0)


def _attn_call(bounded, q, k, v, lq1, lk1, lq2, lk2, subln_gain):
    bsz, t, d = q.shape
    s = k.shape[1]
    tq = ATTN_ROWS
    hp = ATTN_HEADS_PER_STEP
    assert t % (2 * tq) == 0 and HEADS % hp == 0
    const = lambda b, h: (0, 0)
    head = lambda b, h: (b, 0, h)
    return pl.pallas_call(
        functools.partial(_attn_kernel, tq=tq),
        grid=(bsz, HEADS // hp),
        in_specs=[
            pl.BlockSpec(memory_space=pltpu.SMEM),
            pl.BlockSpec((None, t, hp * HEAD_W), head),
            pl.BlockSpec((None, s, hp * HEAD_W), head),
            pl.BlockSpec((None, s, hp * HEAD_W), head),
            pl.BlockSpec((1, MAP_W), const), pl.BlockSpec((1, MAP_W), const),
            pl.BlockSpec((1, MAP_W), const), pl.BlockSpec((1, MAP_W), const),
            pl.BlockSpec((1, HEAD_W), const),
        ],
        out_specs=pl.BlockSpec((None, t, hp * HEAD_W), head),
        out_shape=jax.ShapeDtypeStruct((bsz, t, d), BF16),
        scratch_shapes=[pltpu.VMEM((2, 2, s, tq), BF16), pltpu.VMEM((2, 8, tq), F32),
                        pltpu.VMEM((2, HEAD_W, tq), F32), pltpu.VMEM((hp, HEAD_W, s), BF16)],
        compiler_params=pltpu.CompilerParams(
            dimension_semantics=("arbitrary", "arbitrary"), vmem_limit_bytes=VMEM_LIMIT),
        name="attn",
    )(bounded, q, k, v, lq1, lk1, lq2, lk2, subln_gain)


def _prefix_matrix(c, rev):
    t = np.arange(c)
    m = (t[None, :] >= t[:, None]) if rev else (t[None, :] <= t[:, None])
    return jnp.asarray(m.astype(np.float32), BF16)


def _split2(g):
    hi = g.astype(BF16)
    mid = (g - hi.astype(F32)).astype(BF16)
    return jnp.concatenate([hi, mid], axis=1)


def _cum_decay(l_ref, g, b_scr, rev):
    c, w = g.shape
    e = jnp.dot(l_ref[...], _split2(g), preferred_element_type=F32)
    b = e[:, 0:w] + e[:, w:2 * w]
    b_scr[...] = b
    tot = b_scr[0:1, :] if rev else b_scr[c - 1:c, :]
    return b, tot


def _state_update(st, v, kk, e_to_end, e_total):
    kd = (kk * e_to_end).astype(BF16)
    vt = v.astype(F32).T.astype(BF16)
    return st * e_total + jnp.dot(vt, kd, preferred_element_type=F32)


def _ctx_chunk(st, g, v, l_ref, b_scr, rev):
    b, tot = _cum_decay(l_ref, g, b_scr, rev)
    kk = 1.0 - jnp.exp2(g)
    return _state_update(st, v, kk, jnp.exp2(tot - b), jnp.exp2(tot))


def _level_operand(m, ch):
    qf, kk, eg, b, b_scr, rev = ch["qf"], ch["kk"], ch["eg"], ch["b"], ch["b_scr"], ch["rev"]
    c, w = b.shape
    ref = m if rev else m - 1
    bcast = lambda r, n: jnp.broadcast_to(b_scr[r:r + 1, :], (n, w))
    if m >= 8:
        narrow = m % 16 == 0
        q_, k_ = (ch["qb"], ch["kb"]) if narrow else (qf, kk)
        decay = (lambda e: jnp.exp2(e).astype(BF16)) if narrow else jnp.exp2
        pieces = []
        for j in range(c // (2 * m)):
            lo = slice(j * 2 * m, j * 2 * m + m)
            hi = slice(j * 2 * m + m, (j + 1) * 2 * m)
            r = bcast(j * 2 * m + ref, m)
            if rev:
                pieces += [q_[lo] * decay(b[lo] - r), k_[hi] * decay(r - b[hi])]
            else:
                pieces += [k_[lo] * decay(r - b[lo]), q_[hi] * decay(b[hi] - r)]
        return jnp.concatenate(pieces, axis=0).astype(BF16)
    row = lax.broadcasted_iota(jnp.int32, (c, w), 0)
    upper = (row & m) != 0
    q_rows = jnp.logical_not(upper) if rev else upper
    if m == 1:
        f = jnp.where(q_rows, eg, 1.0)
    else:
        if m == 4:
            r = jnp.concatenate([bcast(8 * j + ref, 8) for j in range(c // 8)], axis=0)
        else:
            sub = lax.broadcasted_iota(jnp.int32, (8, w), 0)
            r = jnp.concatenate([jnp.where(sub < 4, bcast(8 * j + ref, 8), bcast(8 * j + 4 + ref, 8))
                                 for j in range(c // 8)], axis=0)
        f = jnp.exp2(-jnp.abs(b - r))
    return (jnp.where(q_rows, qf, kk) * f).astype(BF16)


def _mid_operands(ch):
    b, b_scr = ch["b"], ch["b_scr"]
    c, w = b.shape
    mid = MID_BLOCK // 2
    r = jnp.concatenate([jnp.broadcast_to(b_scr[j * MID_BLOCK + mid:j * MID_BLOCK + mid + 1, :], (MID_BLOCK, w))
                         for j in range(c // MID_BLOCK)], axis=0)
    d = b - r
    return ch["qb"] * jnp.exp2(d).astype(BF16), ch["kb"] * jnp.exp2(-d).astype(BF16)


def _chunk_decays(ch):
    b, tot = _cum_decay(ch["l_ref"], ch["g"], ch["b_scr"], ch["rev"])
    eg = jnp.exp2(ch["g"])
    kk = 1.0 - eg
    ch.update(b=b, tot=tot, eg=eg, kk=kk, kb=kk.astype(BF16), qb=ch["q"], qf=ch["q"].astype(F32))


def _chunk_state(st, ch, with_diagonal):
    b, tot, kk, qf, v = ch["b"], ch["tot"], ch["kk"], ch["qf"], ch["v"]
    st_new = _state_update(st, v, kk, jnp.exp2(tot - b), jnp.exp2(tot))
    qd = (qf * jnp.exp2(b)).astype(BF16)
    o = lax.dot_general(qd, st.astype(BF16), _NT, preferred_element_type=F32)
    if with_diagonal:
        o = o + jnp.sum(qf * kk, axis=-1, keepdims=True) * v.astype(F32)
    ch["o"] = o
    return st_new


def _chunk_level(ch, m, differ):
    rev = ch["rev"]
    c = ch["b"].shape[0]
    h = c // 2
    if m == "mid":
        xq, xk = _mid_operands(ch)
        block = MID_BLOCK
    else:
        xq = xk = _level_operand(m, ch)
        block = 2 * m
    quadrant = lambda rows, cols_: lax.dot_general(xq[rows], xk[cols_], _NT, preferred_element_type=F32)
    lo, hi = slice(0, h), slice(h, c)
    if block == c:
        ch["a_off"] = (quadrant(lo, hi) if rev else quadrant(hi, lo)).astype(BF16)
    elif block == h:
        ch["a0"], ch["a1"] = quadrant(lo, lo), quadrant(hi, hi)
    else:
        same_block = differ < block
        ch["a0"] = jnp.where(same_block, quadrant(lo, lo), ch["a0"])
        ch["a1"] = jnp.where(same_block, quadrant(hi, hi), ch["a1"])


def _chunk_output(ch, valid):
    v, rev = ch["v"], ch["rev"]
    c = v.shape[0]
    h = c // 2
    a0 = jnp.where(valid, ch["a0"], 0.0).astype(BF16)
    a1 = jnp.where(valid, ch["a1"], 0.0).astype(BF16)
    if rev:
        o_lo = jnp.dot(jnp.concatenate([a0, ch["a_off"]], axis=1), v, preferred_element_type=F32)
        o_hi = jnp.dot(a1, v[h:c], preferred_element_type=F32)
    else:
        o_lo = jnp.dot(a0, v[0:h], preferred_element_type=F32)
        o_hi = jnp.dot(jnp.concatenate([ch["a_off"], a1], axis=1), v, preferred_element_type=F32)
    return ch["o"] + jnp.concatenate([o_lo, o_hi], axis=0)


def _lat_group(chunks, st_f, st_b, mid_ok):
    c = chunks[0]["g"].shape[0]
    h = c // 2
    for ch in chunks:
        _chunk_decays(ch)
    for ch in chunks:
        if ch["rev"]:
            st_b = _chunk_state(st_b, ch, not mid_ok)
        else:
            st_f = _chunk_state(st_f, ch, not mid_ok)
    ti = lax.broadcasted_iota(jnp.int32, (h, h), 0)
    si = lax.broadcasted_iota(jnp.int32, (h, h), 1)
    differ = ti ^ si
    first_lvl = int(math.log2(MID_BLOCK)) if mid_ok else 0
    levels = [1 << lvl for lvl in reversed(range(first_lvl, int(math.log2(c))))] + (["mid"] if mid_ok else [])
    for m in levels:
        for ch in chunks:
            _chunk_level(ch, m, differ)
    if mid_ok:
        valid_f, valid_b = ti >= si, ti <= si
    else:
        valid_f, valid_b = ti > si, ti < si
    outs = [_chunk_output(ch, valid_b if ch["rev"] else valid_f) for ch in chunks]
    return st_f, st_b, outs


def _hgrn_kernel(mid_ref, q_ref, v_ref, gf_ref, gb_ref, lf_ref, lb_ref, hg_ref, o_ref, of_scr, ob_scr, bf_scr,
                 bb_scr, *, lc, c):
    t = q_ref.shape[0]
    dk = gf_ref.shape[1]
    dv = v_ref.shape[1]
    n_ctx = lc // c
    n_lat = t // c
    st0 = jnp.zeros((dv, dk), F32)

    def ctx(j, carry):
        st_f, st_b = carry
        rf = pl.multiple_of(j * c, c)
        rb = pl.multiple_of((n_ctx - 1 - j) * c, c)
        st_f = _ctx_chunk(st_f, gf_ref[pl.ds(rf, c), :], v_ref[pl.ds(rf, c), :], lf_ref, bf_scr.at[0], False)
        st_b = _ctx_chunk(st_b, gb_ref[pl.ds(rb, c), :], v_ref[pl.ds(rb, c), :], lb_ref, bb_scr.at[0], True)
        return st_f, st_b

    carry = lax.fori_loop(0, n_ctx, ctx, (st0, st0))

    def latents(mid_ok):
        def lat(jj, carry):
            st_f, st_b = carry
            chunks = []
            for u in range(LAT_UNROLL):
                j = jj * LAT_UNROLL + u
                rf = pl.multiple_of(j * c, c)
                rb = pl.multiple_of((n_lat - 1 - j) * c, c)
                sf = pl.multiple_of(lc + j * c, c)
                sb = pl.multiple_of(lc + (n_lat - 1 - j) * c, c)
                chunks.append(dict(g=gf_ref[pl.ds(sf, c), :], q=q_ref[pl.ds(rf, c), :], v=v_ref[pl.ds(sf, c), :],
                                   l_ref=lf_ref, b_scr=bf_scr.at[u], rev=False, out=of_scr, row=rf))
                chunks.append(dict(g=gb_ref[pl.ds(sb, c), :], q=q_ref[pl.ds(rb, c), :], v=v_ref[pl.ds(sb, c), :],
                                   l_ref=lb_ref, b_scr=bb_scr.at[u], rev=True, out=ob_scr, row=rb))
            st_f, st_b, outs = _lat_group(chunks, st_f, st_b, mid_ok)
            for ch, o in zip(chunks, outs):
                ch["out"][pl.ds(ch["row"], c), :] = o
            return st_f, st_b

        lax.fori_loop(0, n_lat // LAT_UNROLL, lat, carry)

    @pl.when(mid_ref[0] != 0)
    def _():
        latents(True)

    @pl.when(mid_ref[0] == 0)
    def _():
        latents(False)

    o = of_scr[...] + ob_scr[...]
    ms = jnp.mean(o * o, axis=-1, keepdims=True)
    o_ref[...] = (o * lax.rsqrt(ms + EPS) * hg_ref[...]).astype(o_ref.dtype)


def _hgrn_call(mid_ok, hq, hv, gf, gb, hgrn_gain, lc):
    bsz, t, d = hq.shape
    s = hv.shape[1]
    c = CHUNK
    assert lc % c == 0 and t % c == 0
    lf = _prefix_matrix(c, False)
    lb = _prefix_matrix(c, True)
    const = lambda b, h: (0, 0)
    kern = functools.partial(_hgrn_kernel, lc=lc, c=c)
    return pl.pallas_call(
        kern,
        grid=(bsz, HEADS),
        in_specs=[
            pl.BlockSpec(memory_space=pltpu.SMEM),
            pl.BlockSpec((None, t, HEAD_W), lambda b, h: (b, 0, h)),
            pl.BlockSpec((None, s, HEAD_W), lambda b, h: (b, 0, h)),
            pl.BlockSpec((None, s, HEAD_W), lambda b, h: (b, 0, h)),
            pl.BlockSpec((None, s, HEAD_W), lambda b, h: (b, 0, h)),
            pl.BlockSpec(lf.shape, const), pl.BlockSpec(lb.shape, const),
            pl.BlockSpec((1, HEAD_W), const),
        ],
        out_specs=pl.BlockSpec((None, t, HEAD_W), lambda b, h: (b, 0, h)),
        out_shape=jax.ShapeDtypeStruct((bsz, t, d), BF16),
        scratch_shapes=[pltpu.VMEM((t, HEAD_W), F32), pltpu.VMEM((t, HEAD_W), F32),
                        pltpu.VMEM((LAT_UNROLL, c, HEAD_W), F32), pltpu.VMEM((LAT_UNROLL, c, HEAD_W), F32)],
        compiler_params=pltpu.CompilerParams(
            dimension_semantics=("arbitrary", "arbitrary"), vmem_limit_bytes=VMEM_LIMIT),
        name="hgrn",
    )(mid_ok, hq, hv, gf, gb, lf, lb, hgrn_gain)


def _merge_kernel(oa_ref, oh_ref, za_ref, zh_ref, mg_ref, x_ref, mod_ref, wba_ref, wbh_ref, wo_ref, out_ref, *, d):
    b = pl.program_id(0)
    ya = oa_ref[...] * za_ref[...]
    yh = oh_ref[...] * zh_ref[...]
    ta = jnp.dot(ya, wba_ref[...], preferred_element_type=F32)
    th = jnp.dot(yh, wbh_ref[...], preferred_element_type=F32)
    mix = mg_ref[:, 0:d].astype(F32) * ta + mg_ref[:, d:2 * d].astype(F32) * th
    y = jnp.dot(mix.astype(BF16), wo_ref[...], preferred_element_type=F32)
    gate = mod_ref[pl.ds(b, 1), 2 * d:3 * d]
    out_ref[...] = x_ref[...] + gate * y


def _merge_call(oa, oh, za, zh, mg, x, mod, wba, wbh, wo):
    bsz, t, d = x.shape
    tm = MERGE_ROWS
    tok = lambda b, i: (b, i, 0)
    const = lambda b, i: (0, 0)
    return pl.pallas_call(
        functools.partial(_merge_kernel, d=d),
        grid=(bsz, t // tm),
        in_specs=[
            pl.BlockSpec((None, tm, d), tok), pl.BlockSpec((None, tm, d), tok),
            pl.BlockSpec((None, tm, d), tok), pl.BlockSpec((None, tm, d), tok),
            pl.BlockSpec((None, tm, 2 * d), tok), pl.BlockSpec((None, tm, d), tok),
            pl.BlockSpec(mod.shape, const),
            pl.BlockSpec((d, d), const), pl.BlockSpec((d, d), const), pl.BlockSpec((d, d), const),
        ],
        out_specs=pl.BlockSpec((None, tm, d), tok),
        out_shape=jax.ShapeDtypeStruct((bsz, t, d), F32),
        compiler_params=pltpu.CompilerParams(
            dimension_semantics=("arbitrary", "arbitrary"), vmem_limit_bytes=VMEM_LIMIT),
        name="merge",
    )(oa, oh, za, zh, mg, x, mod, wba, wbh, wo)


def _rope_tables(t, lc):
    rows = t // GRID_W
    row = jnp.repeat(jnp.arange(rows, dtype=F32), GRID_W)
    col = jnp.tile(jnp.arange(GRID_W, dtype=F32), rows)
    half = MAP_W // 2
    inv = ROPE_BASE ** (-jnp.arange(0, half, 2, dtype=F32) / half)
    ar = row[:, None] * inv[None, :]
    ac = col[:, None] * inv[None, :]
    ang = jnp.concatenate([ar, ar, ac, ac], axis=-1)
    ang = jnp.concatenate([ang, ang], axis=-1)
    cos, sin = jnp.cos(ang), jnp.sin(ang)
    first = (jnp.arange(HEAD_W) % (half)) < (half // 2)
    sin_a = jnp.where(first[None, :], -sin, 0.0)
    sin_b = jnp.where(first[None, :], 0.0, sin)
    pad = lambda a, v: jnp.concatenate([jnp.full((lc, HEAD_W), v, F32), a], axis=0)
    return pad(cos, 1.0), pad(sin_a, 0.0), pad(sin_b, 0.0)


def kernel(x, c, ctx, c_ctx, w_mod, b_mod, norm_gain, w_in, q_norm_gain, k_norm_gain, lambda_q1, lambda_k1,
           lambda_q2, lambda_k2, subln_gain, hgrn_lb_fwd, hgrn_lb_bwd, hgrn_norm_gain, w_br_attn, w_br_hgrn,
           w_out):
    bsz, t, d = x.shape
    lc = ctx.shape[1]
    assert w_mod.shape[0] == 1, "single-layer problem"

    mod_rows = ((bsz + 1 + 7) // 8) * 8
    cc = jnp.concatenate([c, c_ctx[None, :], jnp.zeros((mod_rows - bsz - 1, d), F32)], axis=0)
    mod = _mod_call(cc, w_mod[0], b_mod[0][None, :])

    cos_t, sin_a, sin_b = _rope_tables(t, lc)
    q_gain = jnp.tile(q_norm_gain[0], 2)[None, :] * (MAP_W ** -0.5 * LOG2E)
    k_gain = jnp.tile(k_norm_gain[0], 2)[None, :]
    score_bound = (1.03 * MAP_W) * jnp.max(jnp.abs(q_gain)) * jnp.max(jnp.abs(k_gain))
    bounded = (score_bound <= MAX_SOFTMAX_SHIFT).astype(jnp.int32).reshape(1)
    lane = np.arange(256)
    gsum = jnp.asarray((lane[:, None] // MAP_W) == (lane[None, :] // MAP_W), BF16)
    k_a, v_a, v_h, g_f, g_b, q_a, q_h, z_a, z_h, m_g = _proj_call(
        x, ctx, mod, norm_gain, w_in[0].astype(BF16), q_gain, k_gain, cos_t, sin_a, sin_b,
        hgrn_lb_fwd, hgrn_lb_bwd, gsum)

    o_a = _attn_call(bounded, q_a, k_a, v_a, lambda_q1, lambda_k1, lambda_q2, lambda_k2, subln_gain)
    lb_min = jnp.minimum(jnp.min(jax.nn.softmax(hgrn_lb_fwd.astype(F32), axis=0)[0]),
                         jnp.min(jax.nn.softmax(hgrn_lb_bwd.astype(F32), axis=0)[0]))
    mid_ok = (-jnp.log2(lb_min) * (MID_BLOCK // 2) <= MID_MAX_EXPONENT).astype(jnp.int32).reshape(1)
    o_h = _hgrn_call(mid_ok, q_h, v_h, g_f, g_b, hgrn_norm_gain, lc)
    return _merge_call(o_a, o_h, z_a, z_h, m_g, x, mod,
                       w_br_attn[0].astype(BF16), w_br_hgrn[0].astype(BF16), w_out[0].astype(BF16))
```

```python
import functools
import math

import jax
import jax.numpy as jnp
import numpy as np
from jax import lax
from jax.experimental import pallas as pl
from jax.experimental.pallas import tpu as pltpu

F32 = jnp.float32
BF16 = jnp.bfloat16

HEADS = 8
HEAD_W = 128
MAP_W = 64
GRID_W = 64
ROPE_BASE = 10000.0
EPS = 1e-6
LOG2E = 1.4426950408889634
MAX_SOFTMAX_SHIFT = 50.0
LAM_INIT = 0.8 - 0.6 * math.exp(-0.3 * 0)

PROJ_ROWS = 256
PROJ_COLS = 512
PROJ_BATCH = 2
PROJ_GROUPS = 7
ATTN_ROWS = 256
ATTN_HEADS_PER_STEP = 1
ATTN_ROUNDS_PER_BODY = 2
KEY_CHUNK = 1024
MERGE_ROWS = 512
CHUNK = 256
LAT_UNROLL = 4
MID_BLOCK = 32
MID_MAX_EXPONENT = 100.0
VMEM_LIMIT = 56 * 1024 * 1024

G_ATTN_K, G_ATTN_V, G_HGRN_I, G_F_FWD, G_F_BWD, G_ATTN_Q, G_HGRN_Q, G_ATTN_Z, G_HGRN_Z, G_MERGE = range(10)

_NT = (((1,), (1,)), ((), ()))


def _sigmoid(a):
    return 1.0 / (1.0 + jnp.exp(-a))


def _mod_kernel(c_ref, w_ref, b_ref, o_ref):
    a = c_ref[...]
    s = a * _sigmoid(a)
    o_ref[...] = jnp.dot(s, w_ref[...], precision=lax.Precision.HIGHEST,
                         preferred_element_type=F32) + b_ref[...]


def _mod_call(cc, w_mod, b_mod):
    rows, d = cc.shape
    n = w_mod.shape[1]
    bn = 1024
    return pl.pallas_call(
        _mod_kernel,
        grid=(n // bn,),
        in_specs=[pl.BlockSpec((rows, d), lambda j: (0, 0)),
                  pl.BlockSpec((d, bn), lambda j: (0, j)),
                  pl.BlockSpec((1, bn), lambda j: (0, j))],
        out_specs=pl.BlockSpec((rows, bn), lambda j: (0, j)),
        out_shape=jax.ShapeDtypeStruct((rows, n), F32),
        name="mod",
    )(cc, w_mod, b_mod)


def _proj_kernel(x_ref, ctx_ref, mod_ref, ng_ref, w_ref, qg_ref, kg_ref, cos_ref, sa_ref, sb_ref,
                 lbf_ref, lbb_ref, gsum_ref,
                 k_out, v_out, hv_out, gf_out, gb_out, q_out, hq_out, h_out,
                 *, n_ctx_tiles, ctx_row, d):
    bp = pl.program_id(0)
    i = pl.program_id(1)
    is_ctx = i < n_ctx_tiles
    nb, tm = x_ref.shape[0], x_ref.shape[1]

    parts = []
    for r in range(nb):
        u = jnp.where(is_ctx, ctx_ref[r], x_ref[r])
        row = jnp.where(is_ctx, ctx_row, bp * nb + r)
        shift = mod_ref[pl.ds(row, 1), 0:d]
        scale = mod_ref[pl.ds(row, 1), d:2 * d]
        ms = jnp.mean(u * u, axis=-1, keepdims=True)
        y = u * lax.rsqrt(ms + EPS) * ng_ref[...]
        parts.append((y * (1.0 + scale) + shift).astype(BF16))
    hb = jnp.concatenate(parts, axis=0)
    per_row = lambda ref: jnp.concatenate([ref[...]] * nb, axis=0)

    def qk_norm_rope(p, gain_ref):
        p2 = (p * p).astype(BF16)
        cos, sa, sb = per_row(cos_ref), per_row(sa_ref), per_row(sb_ref)
        outs = []
        for cch in range(p.shape[1] // 256):
            ss = jnp.dot(p2[:, cch * 256:(cch + 1) * 256], gsum_ref[...], preferred_element_type=F32)
            inv = lax.rsqrt(ss * (1.0 / MAP_W) + EPS)
            for hh in range(2):
                lo = cch * 256 + hh * HEAD_W
                un = p[:, lo:lo + HEAD_W] * inv[:, hh * HEAD_W:(hh + 1) * HEAD_W] * gain_ref[...]
                outs.append(un * cos + pltpu.roll(un, HEAD_W - 16, 1) * sa + pltpu.roll(un, 16, 1) * sb)
        return jnp.concatenate(outs, axis=1)

    def lower_bound(ref, lo):
        p = ref[:, lo:lo + PROJ_COLS]
        e = jnp.exp(p - jnp.max(p, axis=0, keepdims=True))
        return e[0:1] / jnp.sum(e, axis=0, keepdims=True)

    def log_forget(a, lb):
        return jnp.log(lb + (1.0 - lb) * _sigmoid(a)) * LOG2E

    silu = lambda a, lo: a * _sigmoid(a)
    common = [
        (G_ATTN_K, k_out, lambda a, lo: qk_norm_rope(a, kg_ref)),
        (G_ATTN_V, v_out, lambda a, lo: a),
        (G_HGRN_I, hv_out, lambda a, lo: a),
        (G_F_FWD, gf_out, lambda a, lo: log_forget(a, lower_bound(lbf_ref, lo))),
        (G_F_BWD, gb_out, lambda a, lo: log_forget(a, lower_bound(lbb_ref, lo))),
    ]
    latent_only = [
        (G_ATTN_Q, q_out, lambda a, lo: qk_norm_rope(a, qg_ref)),
        (G_HGRN_Q, hq_out, silu),
    ]

    def run_groups(groups):
        for g, out_ref, epilogue in groups:
            for lo in range(0, out_ref.shape[2], PROJ_COLS):
                col = g * 1024 + lo
                a = jnp.dot(hb, w_ref[:, col:col + PROJ_COLS], preferred_element_type=F32)
                res = epilogue(a, lo).astype(out_ref.dtype)
                for r in range(nb):
                    out_ref[r, :, lo:lo + PROJ_COLS] = res[r * tm:(r + 1) * tm]

    run_groups(common)

    @pl.when(jnp.logical_not(is_ctx))
    def _():
        run_groups(latent_only)
        for r in range(nb):
            h_out[r] = parts[r]


def _proj_call(x, ctx, mod, norm_gain, w_bf, q_gain, k_gain, cos_t, sin_a, sin_b, lb_f, lb_b, gsum):
    bsz, t, d = x.shape
    lc = ctx.shape[1]
    tm = PROJ_ROWS
    nb = PROJ_BATCH
    assert lc % tm == 0 and t % tm == 0 and bsz % nb == 0
    nct = lc // tm
    s = lc + t
    n_tiles = s // tm
    lat = lambda b, i: (b, jnp.maximum(i - nct, 0), 0)
    cat = lambda b, i: (b, i, 0)
    const2 = lambda b, i: (0, 0)
    kern = functools.partial(_proj_kernel, n_ctx_tiles=nct, ctx_row=bsz, d=d)
    bf = lambda n, w: jax.ShapeDtypeStruct((bsz, n, w), BF16)
    return pl.pallas_call(
        kern,
        grid=(bsz // nb, n_tiles),
        in_specs=[
            pl.BlockSpec((nb, tm, d), lat),
            pl.BlockSpec((nb, tm, d), lambda b, i: (b, jnp.minimum(i, nct - 1), 0)),
            pl.BlockSpec(mod.shape, const2),
            pl.BlockSpec((1, d), const2),
            pl.BlockSpec((d, PROJ_GROUPS * 1024), const2, pipeline_mode=pl.Buffered(1)),
            pl.BlockSpec((1, HEAD_W), const2),
            pl.BlockSpec((1, HEAD_W), const2),
            pl.BlockSpec((tm, HEAD_W), lambda b, i: (i, 0)),
            pl.BlockSpec((tm, HEAD_W), lambda b, i: (i, 0)),
            pl.BlockSpec((tm, HEAD_W), lambda b, i: (i, 0)),
            pl.BlockSpec(lb_f.shape, const2),
            pl.BlockSpec(lb_b.shape, const2),
            pl.BlockSpec(gsum.shape, const2),
        ],
        out_specs=[
            pl.BlockSpec((nb, tm, d), cat),
            pl.BlockSpec((nb, tm, d), cat),
            pl.BlockSpec((nb, tm, d), cat),
            pl.BlockSpec((nb, tm, d), cat),
            pl.BlockSpec((nb, tm, d), cat),
            pl.BlockSpec((nb, tm, d), lat),
            pl.BlockSpec((nb, tm, d), lat),
            pl.BlockSpec((nb, tm, d), lat),
        ],
        out_shape=[bf(s, d), bf(s, d), bf(s, d),
                   jax.ShapeDtypeStruct((bsz, s, d), F32), jax.ShapeDtypeStruct((bsz, s, d), F32),
                   bf(t, d), bf(t, d), bf(t, d)],
        compiler_params=pltpu.CompilerParams(
            dimension_semantics=("arbitrary", "arbitrary"), vmem_limit_bytes=VMEM_LIMIT),
        name="proj",
    )(x, ctx, mod, norm_gain, w_bf, q_gain, k_gain, cos_t, sin_a, sin_b, lb_f, lb_b, gsum)


def _attn_kernel(bounded_ref, q_ref, k_ref, v_ref, lq1_ref, lk1_ref, lq2_ref, lk2_ref, sg_ref, o_ref,
                 e_scr, c_scr, acc_scr, vt_scr, *, tq):
    s = k_ref.shape[0]
    n_tiles = q_ref.shape[0] // tq
    n_heads = q_ref.shape[1] // HEAD_W
    chunks = [(st, min(KEY_CHUNK, s - st)) for st in range(0, s, KEY_CHUNK)]
    cols = lambda hd: slice(hd * HEAD_W, (hd + 1) * HEAD_W)
    for hd in range(n_heads):
        vt_scr[hd] = v_ref[:, cols(hd)].astype(F32).T.astype(BF16)
    lam = (jnp.exp(jnp.sum(lq1_ref[...] * lk1_ref[...], axis=-1, keepdims=True))
           - jnp.exp(jnp.sum(lq2_ref[...] * lk2_ref[...], axis=-1, keepdims=True)) + LAM_INIT)

    def map_queries(hd, i):
        q = q_ref[pl.ds(pl.multiple_of(i * tq, tq), tq), cols(hd)]
        lane = lax.broadcasted_iota(jnp.int32, q.shape, 1)
        zero = jnp.zeros_like(q)
        return jnp.where(lane < MAP_W, q, zero), jnp.where(lane >= MAP_W, q, zero)

    def scores(kc, qm):
        return lax.dot_general(kc, qm, _NT, preferred_element_type=F32)

    def stage_a_bounded(hd, i, slot):
        qs = map_queries(hd, i)
        sums = [jnp.zeros((1, tq), F32), jnp.zeros((1, tq), F32)]
        for st, sz in chunks:
            kc = k_ref[st:st + sz, cols(hd)]
            for m in range(2):
                e = jnp.exp2(scores(kc, qs[m]))
                sums[m] = sums[m] + jnp.sum(e, axis=0, keepdims=True)
                e_scr[slot, m, st:st + sz, :] = e.astype(BF16)
            yield
        c_scr[slot, 0:1, :] = 1.0 / sums[0]
        c_scr[slot, 1:2, :] = lam / sums[1]

    def run(*stages):
        live = list(stages)
        while live:
            live = [g for g in live if next(g, StopIteration) is not StopIteration]

    def stage_a_general(hd, i, slot):
        qs = map_queries(hd, i)
        k = k_ref[:, cols(hd)]
        for m in range(2):
            sm = scores(k, qs[m])
            e = jnp.exp2(sm - jnp.max(sm, axis=0, keepdims=True))
            c_scr[slot, m:m + 1, :] = (lam if m else 1.0) / jnp.sum(e, axis=0, keepdims=True)
            e_scr[slot, m] = e.astype(BF16)

    def stage_b(hd, i, slot):
        c1 = c_scr[slot, 0:1, :].astype(BF16)
        c2 = c_scr[slot, 1:2, :].astype(BF16)
        acc = jnp.zeros((HEAD_W, tq), F32)
        for st, sz in chunks:
            w = e_scr[slot, 0, st:st + sz, :] * c1 - e_scr[slot, 1, st:st + sz, :] * c2
            acc = acc + jnp.dot(vt_scr[hd, :, st:st + sz], w, preferred_element_type=F32)
            yield
        acc_scr[slot] = acc

    def stage_c(hd, i, slot):
        o = acc_scr[slot].T
        ms = jnp.mean(o * o, axis=-1, keepdims=True)
        o = o * lax.rsqrt(ms + EPS) * sg_ref[...] * (1.0 - LAM_INIT)
        o_ref[pl.ds(pl.multiple_of(i * tq, tq), tq), cols(hd)] = o.astype(o_ref.dtype)
        yield

    @pl.when(bounded_ref[0] != 0)
    def _():
        last = n_tiles - 1
        run(stage_a_bounded(0, 0, 0))
        run(stage_a_bounded(0, 1, 1), stage_b(0, 0, 0))
        n_steady = n_tiles - 2
        per_body = ATTN_ROUNDS_PER_BODY
        for hd in range(n_heads):
            def steady(i, parity, hd=hd):
                run(stage_c(hd, i, parity), stage_a_bounded(hd, i + 2, parity), stage_b(hd, i + 1, 1 - parity))

            def body(j, _, steady=steady):
                for r in range(per_body):
                    steady(per_body * j + r, r % 2)
                return 0

            lax.fori_loop(0, n_steady // per_body, body, 0)
            for i in range(n_steady - n_steady % per_body, n_steady):
                steady(i, i % 2)
            if hd + 1 < n_heads:
                run(stage_c(hd, last - 1, 0), stage_a_bounded(hd + 1, 0, 0), stage_b(hd, last, 1))
                run(stage_c(hd, last, 1), stage_a_bounded(hd + 1, 1, 1), stage_b(hd + 1, 0, 0))
            else:
                run(stage_c(hd, last - 1, 0), stage_b(hd, last, 1))
                run(stage_c(hd, last, 1))

    @pl.when(bounded_ref[0] == 0)
    def _():
        for hd in range(n_heads):
            def tile(i, _, hd=hd):
                stage_a_general(hd, i, 0)
                run(stage_b(hd, i, 0))
                run(stage_c(hd, i, 0))
                return 0

            lax.fori_loop(0, n_tiles, tile, 0)


def _attn_call(bounded, q, k, v, lq1, lk1, lq2, lk2, subln_gain):
    bsz, t, d = q.shape
    s = k.shape[1]
    tq = ATTN_ROWS
    hp = ATTN_HEADS_PER_STEP
    assert t % (2 * tq) == 0 and HEADS % hp == 0
    const = lambda b, h: (0, 0)
    head = lambda b, h: (b, 0, h)
    return pl.pallas_call(
        functools.partial(_attn_kernel, tq=tq),
        grid=(bsz, HEADS // hp),
        in_specs=[
            pl.BlockSpec(memory_space=pltpu.SMEM),
            pl.BlockSpec((None, t, hp * HEAD_W), head),
            pl.BlockSpec((None, s, hp * HEAD_W), head),
            pl.BlockSpec((None, s, hp * HEAD_W), head),
            pl.BlockSpec((1, MAP_W), const), pl.BlockSpec((1, MAP_W), const),
            pl.BlockSpec((1, MAP_W), const), pl.BlockSpec((1, MAP_W), const),
            pl.BlockSpec((1, HEAD_W), const),
        ],
        out_specs=pl.BlockSpec((None, t, hp * HEAD_W), head),
        out_shape=jax.ShapeDtypeStruct((bsz, t, d), BF16),
        scratch_shapes=[pltpu.VMEM((2, 2, s, tq), BF16), pltpu.VMEM((2, 8, tq), F32),
                        pltpu.VMEM((2, HEAD_W, tq), F32), pltpu.VMEM((hp, HEAD_W, s), BF16)],
        compiler_params=pltpu.CompilerParams(
            dimension_semantics=("arbitrary", "arbitrary"), vmem_limit_bytes=VMEM_LIMIT),
        name="attn",
    )(bounded, q, k, v, lq1, lk1, lq2, lk2, subln_gain)


def _prefix_matrix(c, rev):
    t = np.arange(c)
    m = (t[None, :] >= t[:, None]) if rev else (t[None, :] <= t[:, None])
    return jnp.asarray(m.astype(np.float32), BF16)


def _split2(g):
    hi = g.astype(BF16)
    mid = (g - hi.astype(F32)).astype(BF16)
    return jnp.concatenate([hi, mid], axis=1)


def _cum_decay(l_ref, g, b_scr, rev):
    c, w = g.shape
    e = jnp.dot(l_ref[...], _split2(g), preferred_element_type=F32)
    b = e[:, 0:w] + e[:, w:2 * w]
    b_scr[...] = b
    tot = b_scr[0:1, :] if rev else b_scr[c - 1:c, :]
    return b, tot


def _state_update(st, v, kk, e_to_end, e_total):
    kd = (kk * e_to_end).astype(BF16)
    vt = v.astype(F32).T.astype(BF16)
    return st * e_total + jnp.dot(vt, kd, preferred_element_type=F32)


def _ctx_pair(st_f, st_b, gf, vf, gb, vb, lf_ref, lb_ref, bf_scr, bb_scr):
    bf, tot_f = _cum_decay(lf_ref, gf, bf_scr, False)
    bb, tot_b = _cum_decay(lb_ref, gb, bb_scr, True)
    st_f = _state_update(st_f, vf, 1.0 - jnp.exp2(gf), jnp.exp2(tot_f - bf), jnp.exp2(tot_f))
    st_b = _state_update(st_b, vb, 1.0 - jnp.exp2(gb), jnp.exp2(tot_b - bb), jnp.exp2(tot_b))
    return st_f, st_b


def _level_operand(m, ch):
    qf, kk, eg, b, b_scr, rev = ch["qf"], ch["kk"], ch["eg"], ch["b"], ch["b_scr"], ch["rev"]
    c, w = b.shape
    ref = m if rev else m - 1
    bcast = lambda r, n: jnp.broadcast_to(b_scr[r:r + 1, :], (n, w))
    if m >= 8:
        narrow = m % 16 == 0
        q_, k_ = (ch["qb"], ch["kb"]) if narrow else (qf, kk)
        decay = (lambda e: jnp.exp2(e).astype(BF16)) if narrow else jnp.exp2
        pieces = []
        for j in range(c // (2 * m)):
            lo = slice(j * 2 * m, j * 2 * m + m)
            hi = slice(j * 2 * m + m, (j + 1) * 2 * m)
            r = bcast(j * 2 * m + ref, m)
            if rev:
                pieces += [q_[lo] * decay(b[lo] - r), k_[hi] * decay(r - b[hi])]
            else:
                pieces += [k_[lo] * decay(r - b[lo]), q_[hi] * decay(b[hi] - r)]
        return jnp.concatenate(pieces, axis=0).astype(BF16)
    row = lax.broadcasted_iota(jnp.int32, (c, w), 0)
    upper = (row & m) != 0
    q_rows = jnp.logical_not(upper) if rev else upper
    if m == 1:
        f = jnp.where(q_rows, eg, 1.0)
    else:
        if m == 4:
            r = jnp.concatenate([bcast(8 * j + ref, 8) for j in range(c // 8)], axis=0)
        else:
            sub = lax.broadcasted_iota(jnp.int32, (8, w), 0)
            r = jnp.concatenate([jnp.where(sub < 4, bcast(8 * j + ref, 8), bcast(8 * j + 4 + ref, 8))
                                 for j in range(c // 8)], axis=0)
        f = jnp.exp2(-jnp.abs(b - r))
    return (jnp.where(q_rows, qf, kk) * f).astype(BF16)


def _mid_operands(ch):
    b, b_scr = ch["b"], ch["b_scr"]
    c, w = b.shape
    mid = MID_BLOCK // 2
    r = jnp.concatenate([jnp.broadcast_to(b_scr[j * MID_BLOCK + mid:j * MID_BLOCK + mid + 1, :], (MID_BLOCK, w))
                         for j in range(c // MID_BLOCK)], axis=0)
    d = b - r
    return ch["qb"] * jnp.exp2(d).astype(BF16), ch["kb"] * jnp.exp2(-d).astype(BF16)


def _chunk_decays(ch):
    b, tot = _cum_decay(ch["l_ref"], ch["g"], ch["b_scr"], ch["rev"])
    eg = jnp.exp2(ch["g"])
    kk = 1.0 - eg
    ch.update(b=b, tot=tot, eg=eg, kk=kk, kb=kk.astype(BF16), qb=ch["q"], qf=ch["q"].astype(F32))


def _chunk_state(st, ch, with_diagonal):
    b, tot, kk, qf, v = ch["b"], ch["tot"], ch["kk"], ch["qf"], ch["v"]
    st_new = _state_update(st, v, kk, jnp.exp2(tot - b), jnp.exp2(tot))
    qd = (qf * jnp.exp2(b)).astype(BF16)
    o = lax.dot_general(qd, st.astype(BF16), _NT, preferred_element_type=F32)
    if with_diagonal:
        o = o + jnp.sum(qf * kk, axis=-1, keepdims=True) * v.astype(F32)
    ch["o"] = o
    return st_new


def _chunk_level(ch, m, differ):
    rev = ch["rev"]
    c = ch["b"].shape[0]
    h = c // 2
    if m == "mid":
        xq, xk = _mid_operands(ch)
        block = MID_BLOCK
    else:
        xq = xk = _level_operand(m, ch)
        block = 2 * m
    quadrant = lambda rows, cols_: lax.dot_general(xq[rows], xk[cols_], _NT, preferred_element_type=F32)
    lo, hi = slice(0, h), slice(h, c)
    if block == c:
        ch["a_off"] = (quadrant(lo, hi) if rev else quadrant(hi, lo)).astype(BF16)
    elif block == h:
        ch["a0"], ch["a1"] = quadrant(lo, lo), quadrant(hi, hi)
    else:
        same_block = differ < block
        ch["a0"] = jnp.where(same_block, quadrant(lo, lo), ch["a0"])
        ch["a1"] = jnp.where(same_block, quadrant(hi, hi), ch["a1"])


def _chunk_output(ch, valid):
    v, rev = ch["v"], ch["rev"]
    c = v.shape[0]
    h = c // 2
    a0 = jnp.where(valid, ch["a0"], 0.0).astype(BF16)
    a1 = jnp.where(valid, ch["a1"], 0.0).astype(BF16)
    if rev:
        o_lo = jnp.dot(jnp.concatenate([a0, ch["a_off"]], axis=1), v, preferred_element_type=F32)
        o_hi = jnp.dot(a1, v[h:c], preferred_element_type=F32)
    else:
        o_lo = jnp.dot(a0, v[0:h], preferred_element_type=F32)
        o_hi = jnp.dot(jnp.concatenate([ch["a_off"], a1], axis=1), v, preferred_element_type=F32)
    return ch["o"] + jnp.concatenate([o_lo, o_hi], axis=0)


def _lat_group(chunks, st_f, st_b, mid_ok):
    c = chunks[0]["g"].shape[0]
    h = c // 2
    for ch in chunks:
        _chunk_decays(ch)
    for ch in chunks:
        if ch["rev"]:
            st_b = _chunk_state(st_b, ch, not mid_ok)
        else:
            st_f = _chunk_state(st_f, ch, not mid_ok)
    ti = lax.broadcasted_iota(jnp.int32, (h, h), 0)
    si = lax.broadcasted_iota(jnp.int32, (h, h), 1)
    differ = ti ^ si
    first_lvl = int(math.log2(MID_BLOCK)) if mid_ok else 0
    levels = [1 << lvl for lvl in reversed(range(first_lvl, int(math.log2(c))))] + (["mid"] if mid_ok else [])
    for m in levels:
        for ch in chunks:
            _chunk_level(ch, m, differ)
    if mid_ok:
        valid_f, valid_b = ti >= si, ti <= si
    else:
        valid_f, valid_b = ti > si, ti < si
    outs = [_chunk_output(ch, valid_b if ch["rev"] else valid_f) for ch in chunks]
    return st_f, st_b, outs


def _hgrn_kernel(mid_ref, q_ref, v_ref, gf_ref, gb_ref, lf_ref, lb_ref, hg_ref, o_ref, of_scr, ob_scr, bf_scr,
                 bb_scr, *, lc, c):
    t = q_ref.shape[0]
    dk = gf_ref.shape[1]
    dv = v_ref.shape[1]
    n_ctx = lc // c
    n_lat = t // c
    st0 = jnp.zeros((dv, dk), F32)

    def ctx(j, carry):
        st_f, st_b = carry
        rf = pl.multiple_of(j * c, c)
        rb = pl.multiple_of((n_ctx - 1 - j) * c, c)
        return _ctx_pair(st_f, st_b, gf_ref[pl.ds(rf, c), :], v_ref[pl.ds(rf, c), :], gb_ref[pl.ds(rb, c), :],
                         v_ref[pl.ds(rb, c), :], lf_ref, lb_ref, bf_scr.at[0], bb_scr.at[0])

    carry = lax.fori_loop(0, n_ctx, ctx, (st0, st0))

    def latents(mid_ok):
        def lat(jj, carry):
            st_f, st_b = carry
            chunks = []
            for u in range(LAT_UNROLL):
                j = jj * LAT_UNROLL + u
                rf = pl.multiple_of(j * c, c)
                rb = pl.multiple_of((n_lat - 1 - j) * c, c)
                sf = pl.multiple_of(lc + j * c, c)
                sb = pl.multiple_of(lc + (n_lat - 1 - j) * c, c)
                chunks.append(dict(g=gf_ref[pl.ds(sf, c), :], q=q_ref[pl.ds(rf, c), :], v=v_ref[pl.ds(sf, c), :],
                                   l_ref=lf_ref, b_scr=bf_scr.at[u], rev=False, out=of_scr, row=rf))
                chunks.append(dict(g=gb_ref[pl.ds(sb, c), :], q=q_ref[pl.ds(rb, c), :], v=v_ref[pl.ds(sb, c), :],
                                   l_ref=lb_ref, b_scr=bb_scr.at[u], rev=True, out=ob_scr, row=rb))
            st_f, st_b, outs = _lat_group(chunks, st_f, st_b, mid_ok)
            for ch, o in zip(chunks, outs):
                ch["out"][pl.ds(ch["row"], c), :] = o
            return st_f, st_b

        lax.fori_loop(0, n_lat // LAT_UNROLL, lat, carry)

    @pl.when(mid_ref[0] != 0)
    def _():
        latents(True)

    @pl.when(mid_ref[0] == 0)
    def _():
        latents(False)

    o = of_scr[...] + ob_scr[...]
    ms = jnp.mean(o * o, axis=-1, keepdims=True)
    o_ref[...] = (o * lax.rsqrt(ms + EPS) * hg_ref[...]).astype(o_ref.dtype)


def _hgrn_call(mid_ok, hq, hv, gf, gb, hgrn_gain, lc):
    bsz, t, d = hq.shape
    s = hv.shape[1]
    c = CHUNK
    assert lc % c == 0 and t % c == 0
    lf = _prefix_matrix(c, False)
    lb = _prefix_matrix(c, True)
    const = lambda b, h: (0, 0)
    kern = functools.partial(_hgrn_kernel, lc=lc, c=c)
    return pl.pallas_call(
        kern,
        grid=(bsz, HEADS),
        in_specs=[
            pl.BlockSpec(memory_space=pltpu.SMEM),
            pl.BlockSpec((None, t, HEAD_W), lambda b, h: (b, 0, h)),
            pl.BlockSpec((None, s, HEAD_W), lambda b, h: (b, 0, h)),
            pl.BlockSpec((None, s, HEAD_W), lambda b, h: (b, 0, h)),
            pl.BlockSpec((None, s, HEAD_W), lambda b, h: (b, 0, h)),
            pl.BlockSpec(lf.shape, const), pl.BlockSpec(lb.shape, const),
            pl.BlockSpec((1, HEAD_W), const),
        ],
        out_specs=pl.BlockSpec((None, t, HEAD_W), lambda b, h: (b, 0, h)),
        out_shape=jax.ShapeDtypeStruct((bsz, t, d), BF16),
        scratch_shapes=[pltpu.VMEM((t, HEAD_W), F32), pltpu.VMEM((t, HEAD_W), F32),
                        pltpu.VMEM((LAT_UNROLL, c, HEAD_W), F32), pltpu.VMEM((LAT_UNROLL, c, HEAD_W), F32)],
        compiler_params=pltpu.CompilerParams(
            dimension_semantics=("arbitrary", "arbitrary"), vmem_limit_bytes=VMEM_LIMIT),
        name="hgrn",
    )(mid_ok, hq, hv, gf, gb, lf, lb, hgrn_gain)


def _merge_kernel(oa_ref, oh_ref, h_ref, x_ref, mod_ref, wza_ref, wzh_ref, wga_ref, wgh_ref, wba_ref, wbh_ref,
                  wo_ref, out_ref, *, d):
    b = pl.program_id(0)
    h = h_ref[...]
    gate_path = lambda w_ref: jnp.dot(h, w_ref[...], preferred_element_type=F32)
    silu = lambda a: a * _sigmoid(a)
    ya = oa_ref[...] * silu(gate_path(wza_ref)).astype(BF16)
    yh = oh_ref[...] * silu(gate_path(wzh_ref)).astype(BF16)
    ta = jnp.dot(ya, wba_ref[...], preferred_element_type=F32)
    th = jnp.dot(yh, wbh_ref[...], preferred_element_type=F32)
    mix = _sigmoid(gate_path(wga_ref)) * ta + _sigmoid(gate_path(wgh_ref)) * th
    y = jnp.dot(mix.astype(BF16), wo_ref[...], preferred_element_type=F32)
    gate = mod_ref[pl.ds(b, 1), 2 * d:3 * d]
    out_ref[...] = x_ref[...] + gate * y


def _merge_call(oa, oh, h, x, mod, w_in, wba, wbh, wo):
    bsz, t, d = x.shape
    tm = MERGE_ROWS
    tok = lambda b, i: (b, i, 0)
    const = lambda b, i: (0, 0)
    resident = lambda idx=const: pl.BlockSpec((d, d), idx, pipeline_mode=pl.Buffered(1))
    in_group = lambda g: resident(lambda b, i: (0, g))
    return pl.pallas_call(
        functools.partial(_merge_kernel, d=d),
        grid=(bsz, t // tm),
        in_specs=[
            pl.BlockSpec((None, tm, d), tok), pl.BlockSpec((None, tm, d), tok),
            pl.BlockSpec((None, tm, d), tok), pl.BlockSpec((None, tm, d), tok),
            pl.BlockSpec(mod.shape, const),
            in_group(G_ATTN_Z), in_group(G_HGRN_Z), in_group(G_MERGE), in_group(G_MERGE + 1),
            resident(), resident(), resident(),
        ],
        out_specs=pl.BlockSpec((None, tm, d), tok),
        out_shape=jax.ShapeDtypeStruct((bsz, t, d), F32),
        compiler_params=pltpu.CompilerParams(
            dimension_semantics=("arbitrary", "arbitrary"), vmem_limit_bytes=VMEM_LIMIT),
        name="merge",
    )(oa, oh, h, x, mod, w_in, w_in, w_in, w_in, wba, wbh, wo)


def _rope_tables(t, lc):
    rows = t // GRID_W
    row = jnp.repeat(jnp.arange(rows, dtype=F32), GRID_W)
    col = jnp.tile(jnp.arange(GRID_W, dtype=F32), rows)
    half = MAP_W // 2
    inv = ROPE_BASE ** (-jnp.arange(0, half, 2, dtype=F32) / half)
    ar = row[:, None] * inv[None, :]
    ac = col[:, None] * inv[None, :]
    ang = jnp.concatenate([ar, ar, ac, ac], axis=-1)
    ang = jnp.concatenate([ang, ang], axis=-1)
    cos, sin = jnp.cos(ang), jnp.sin(ang)
    first = (jnp.arange(HEAD_W) % (half)) < (half // 2)
    sin_a = jnp.where(first[None, :], -sin, 0.0)
    sin_b = jnp.where(first[None, :], 0.0, sin)
    pad = lambda a, v: jnp.concatenate([jnp.full((lc, HEAD_W), v, F32), a], axis=0)
    return pad(cos, 1.0), pad(sin_a, 0.0), pad(sin_b, 0.0)


def kernel(x, c, ctx, c_ctx, w_mod, b_mod, norm_gain, w_in, q_norm_gain, k_norm_gain, lambda_q1, lambda_k1,
           lambda_q2, lambda_k2, subln_gain, hgrn_lb_fwd, hgrn_lb_bwd, hgrn_norm_gain, w_br_attn, w_br_hgrn,
           w_out):
    bsz, t, d = x.shape
    lc = ctx.shape[1]
    assert w_mod.shape[0] == 1, "single-layer problem"

    mod_rows = ((bsz + 1 + 7) // 8) * 8
    cc = jnp.concatenate([c, c_ctx[None, :], jnp.zeros((mod_rows - bsz - 1, d), F32)], axis=0)
    mod = _mod_call(cc, w_mod[0], b_mod[0][None, :])

    cos_t, sin_a, sin_b = _rope_tables(t, lc)
    q_gain = jnp.tile(q_norm_gain[0], 2)[None, :] * (MAP_W ** -0.5 * LOG2E)
    k_gain = jnp.tile(k_norm_gain[0], 2)[None, :]
    score_bound = (1.03 * MAP_W) * jnp.max(jnp.abs(q_gain)) * jnp.max(jnp.abs(k_gain))
    bounded = (score_bound <= MAX_SOFTMAX_SHIFT).astype(jnp.int32).reshape(1)
    lane = np.arange(256)
    gsum = jnp.asarray((lane[:, None] // MAP_W) == (lane[None, :] // MAP_W), BF16)
    w_bf = w_in[0].astype(BF16)
    k_a, v_a, v_h, g_f, g_b, q_a, q_h, h_lat = _proj_call(
        x, ctx, mod, norm_gain, w_bf, q_gain, k_gain, cos_t, sin_a, sin_b, hgrn_lb_fwd, hgrn_lb_bwd, gsum)

    o_a = _attn_call(bounded, q_a, k_a, v_a, lambda_q1, lambda_k1, lambda_q2, lambda_k2, subln_gain)
    lb_min = jnp.minimum(jnp.min(jax.nn.softmax(hgrn_lb_fwd.astype(F32), axis=0)[0]),
                         jnp.min(jax.nn.softmax(hgrn_lb_bwd.astype(F32), axis=0)[0]))
    mid_ok = (-jnp.log2(lb_min) * (MID_BLOCK // 2) <= MID_MAX_EXPONENT).astype(jnp.int32).reshape(1)
    o_h = _hgrn_call(mid_ok, q_h, v_h, g_f, g_b, hgrn_norm_gain, lc)
    return _merge_call(o_a, o_h, h_lat, x, mod, w_bf,
                       w_br_attn[0].astype(BF16), w_br_hgrn[0].astype(BF16), w_out[0].astype(BF16))
```

```python
import functools
import math

import jax
import jax.numpy as jnp
import numpy as np
from jax import lax
from jax.experimental import pallas as pl
from jax.experimental.pallas import tpu as pltpu

F32 = jnp.float32
BF16 = jnp.bfloat16

HEADS = 8
HEAD_W = 128
MAP_W = 64
GRID_W = 64
ROPE_BASE = 10000.0
EPS = 1e-6
LOG2E = 1.4426950408889634
MAX_SOFTMAX_SHIFT = 50.0
LAM_INIT = 0.8 - 0.6 * math.exp(-0.3 * 0)

PROJ_ROWS = 256
PROJ_COLS = 512
PROJ_BATCH = 2
PROJ_GROUPS = 7
ATTN_ROWS = 256
ATTN_HEADS_PER_STEP = 1
ATTN_ROUNDS_PER_BODY = 2
KEY_CHUNK = 1024
MERGE_ROWS = 512
CHUNK = 256
LAT_UNROLL = 4
MID_BLOCK = 32
MID_MAX_EXPONENT = 100.0
VMEM_LIMIT = 56 * 1024 * 1024

G_ATTN_K, G_ATTN_V, G_HGRN_I, G_F_FWD, G_F_BWD, G_ATTN_Q, G_HGRN_Q, G_ATTN_Z, G_HGRN_Z, G_MERGE = range(10)

_NT = (((1,), (1,)), ((), ()))


def _sigmoid(a):
    return 1.0 / (1.0 + jnp.exp(-a))


def _gate_sigmoid(a):
    return 0.5 + 0.5 * jnp.tanh(0.5 * a)


def _mod_kernel(c_ref, w_ref, b_ref, o_ref):
    a = c_ref[...]
    s = a * _sigmoid(a)
    o_ref[...] = jnp.dot(s, w_ref[...], precision=lax.Precision.HIGHEST,
                         preferred_element_type=F32) + b_ref[...]


def _mod_call(cc, w_mod, b_mod):
    rows, d = cc.shape
    n = w_mod.shape[1]
    bn = 1024
    return pl.pallas_call(
        _mod_kernel,
        grid=(n // bn,),
        in_specs=[pl.BlockSpec((rows, d), lambda j: (0, 0)),
                  pl.BlockSpec((d, bn), lambda j: (0, j)),
                  pl.BlockSpec((1, bn), lambda j: (0, j))],
        out_specs=pl.BlockSpec((rows, bn), lambda j: (0, j)),
        out_shape=jax.ShapeDtypeStruct((rows, n), F32),
        name="mod",
    )(cc, w_mod, b_mod)


def _proj_kernel(x_ref, ctx_ref, mod_ref, ng_ref, w_ref, qg_ref, kg_ref, cos_ref, sa_ref, sb_ref,
                 lbf_ref, lbb_ref, gsum_ref,
                 k_out, v_out, hv_out, gf_out, gb_out, q_out, hq_out, h_out,
                 *, n_ctx_tiles, ctx_row, d):
    bp = pl.program_id(0)
    i = pl.program_id(1)
    is_ctx = i < n_ctx_tiles
    nb, tm = x_ref.shape[0], x_ref.shape[1]

    parts = []
    for r in range(nb):
        u = jnp.where(is_ctx, ctx_ref[r], x_ref[r])
        row = jnp.where(is_ctx, ctx_row, bp * nb + r)
        shift = mod_ref[pl.ds(row, 1), 0:d]
        scale = mod_ref[pl.ds(row, 1), d:2 * d]
        ms = jnp.mean(u * u, axis=-1, keepdims=True)
        y = u * lax.rsqrt(ms + EPS) * ng_ref[...]
        parts.append((y * (1.0 + scale) + shift).astype(BF16))
    hb = jnp.concatenate(parts, axis=0)
    per_row = lambda ref: jnp.concatenate([ref[...]] * nb, axis=0)

    def qk_norm_rope(p, gain_ref):
        p2 = (p * p).astype(BF16)
        cos, sa, sb = per_row(cos_ref), per_row(sa_ref), per_row(sb_ref)
        outs = []
        for cch in range(p.shape[1] // 256):
            ss = jnp.dot(p2[:, cch * 256:(cch + 1) * 256], gsum_ref[...], preferred_element_type=F32)
            inv = lax.rsqrt(ss * (1.0 / MAP_W) + EPS)
            for hh in range(2):
                lo = cch * 256 + hh * HEAD_W
                un = p[:, lo:lo + HEAD_W] * inv[:, hh * HEAD_W:(hh + 1) * HEAD_W] * gain_ref[...]
                outs.append(un * cos + pltpu.roll(un, HEAD_W - 16, 1) * sa + pltpu.roll(un, 16, 1) * sb)
        return jnp.concatenate(outs, axis=1)

    def lower_bound(ref, lo):
        p = ref[:, lo:lo + PROJ_COLS]
        e = jnp.exp(p - jnp.max(p, axis=0, keepdims=True))
        return e[0:1] / jnp.sum(e, axis=0, keepdims=True)

    def log_forget(a, lb):
        return jnp.log(lb + (1.0 - lb) * _sigmoid(a)) * LOG2E

    silu = lambda a, lo: a * _gate_sigmoid(a)
    common = [
        (G_ATTN_K, k_out, lambda a, lo: qk_norm_rope(a, kg_ref)),
        (G_ATTN_V, v_out, lambda a, lo: a),
        (G_HGRN_I, hv_out, lambda a, lo: a),
        (G_F_FWD, gf_out, lambda a, lo: log_forget(a, lower_bound(lbf_ref, lo))),
        (G_F_BWD, gb_out, lambda a, lo: log_forget(a, lower_bound(lbb_ref, lo))),
    ]
    latent_only = [
        (G_ATTN_Q, q_out, lambda a, lo: qk_norm_rope(a, qg_ref)),
        (G_HGRN_Q, hq_out, silu),
    ]

    def run_groups(groups):
        for g, out_ref, epilogue in groups:
            for lo in range(0, out_ref.shape[2], PROJ_COLS):
                col = g * 1024 + lo
                a = jnp.dot(hb, w_ref[:, col:col + PROJ_COLS], preferred_element_type=F32)
                res = epilogue(a, lo).astype(out_ref.dtype)
                for r in range(nb):
                    out_ref[r, :, lo:lo + PROJ_COLS] = res[r * tm:(r + 1) * tm]

    run_groups(common)

    @pl.when(jnp.logical_not(is_ctx))
    def _():
        run_groups(latent_only)
        for r in range(nb):
            h_out[r] = parts[r]


def _proj_call(x, ctx, mod, norm_gain, w_bf, q_gain, k_gain, cos_t, sin_a, sin_b, lb_f, lb_b, gsum):
    bsz, t, d = x.shape
    lc = ctx.shape[1]
    tm = PROJ_ROWS
    nb = PROJ_BATCH
    assert lc % tm == 0 and t % tm == 0 and bsz % nb == 0
    nct = lc // tm
    s = lc + t
    n_tiles = s // tm
    lat = lambda b, i: (b, jnp.maximum(i - nct, 0), 0)
    cat = lambda b, i: (b, i, 0)
    const2 = lambda b, i: (0, 0)
    kern = functools.partial(_proj_kernel, n_ctx_tiles=nct, ctx_row=bsz, d=d)
    bf = lambda n, w: jax.ShapeDtypeStruct((bsz, n, w), BF16)
    return pl.pallas_call(
        kern,
        grid=(bsz // nb, n_tiles),
        in_specs=[
            pl.BlockSpec((nb, tm, d), lat),
            pl.BlockSpec((nb, tm, d), lambda b, i: (b, jnp.minimum(i, nct - 1), 0)),
            pl.BlockSpec(mod.shape, const2),
            pl.BlockSpec((1, d), const2),
            pl.BlockSpec((d, PROJ_GROUPS * 1024), const2, pipeline_mode=pl.Buffered(1)),
            pl.BlockSpec((1, HEAD_W), const2),
            pl.BlockSpec((1, HEAD_W), const2),
            pl.BlockSpec((tm, HEAD_W), lambda b, i: (i, 0)),
            pl.BlockSpec((tm, HEAD_W), lambda b, i: (i, 0)),
            pl.BlockSpec((tm, HEAD_W), lambda b, i: (i, 0)),
            pl.BlockSpec(lb_f.shape, const2),
            pl.BlockSpec(lb_b.shape, const2),
            pl.BlockSpec(gsum.shape, const2),
        ],
        out_specs=[
            pl.BlockSpec((nb, tm, d), cat),
            pl.BlockSpec((nb, tm, d), cat),
            pl.BlockSpec((nb, tm, d), cat),
            pl.BlockSpec((nb, tm, d), cat),
            pl.BlockSpec((nb, tm, d), cat),
            pl.BlockSpec((nb, tm, d), lat),
            pl.BlockSpec((nb, tm, d), lat),
            pl.BlockSpec((nb, tm, d), lat),
        ],
        out_shape=[bf(s, d), bf(s, d), bf(s, d),
                   jax.ShapeDtypeStruct((bsz, s, d), F32), jax.ShapeDtypeStruct((bsz, s, d), F32),
                   bf(t, d), bf(t, d), bf(t, d)],
        compiler_params=pltpu.CompilerParams(
            dimension_semantics=("arbitrary", "arbitrary"), vmem_limit_bytes=VMEM_LIMIT),
        name="proj",
    )(x, ctx, mod, norm_gain, w_bf, q_gain, k_gain, cos_t, sin_a, sin_b, lb_f, lb_b, gsum)


def _attn_kernel(bounded_ref, q_ref, k_ref, v_ref, lq1_ref, lk1_ref, lq2_ref, lk2_ref, sg_ref, o_ref,
                 e_scr, c_scr, acc_scr, vt_scr, *, tq):
    s = k_ref.shape[0]
    n_tiles = q_ref.shape[0] // tq
    n_heads = q_ref.shape[1] // HEAD_W
    chunks = [(st, min(KEY_CHUNK, s - st)) for st in range(0, s, KEY_CHUNK)]
    cols = lambda hd: slice(hd * HEAD_W, (hd + 1) * HEAD_W)
    for hd in range(n_heads):
        vt_scr[hd] = v_ref[:, cols(hd)].astype(F32).T.astype(BF16)
    lam = (jnp.exp(jnp.sum(lq1_ref[...] * lk1_ref[...], axis=-1, keepdims=True))
           - jnp.exp(jnp.sum(lq2_ref[...] * lk2_ref[...], axis=-1, keepdims=True)) + LAM_INIT)

    def map_queries(hd, i):
        q = q_ref[pl.ds(pl.multiple_of(i * tq, tq), tq), cols(hd)]
        lane = lax.broadcasted_iota(jnp.int32, q.shape, 1)
        zero = jnp.zeros_like(q)
        return jnp.where(lane < MAP_W, q, zero), jnp.where(lane >= MAP_W, q, zero)

    def scores(kc, qm):
        return lax.dot_general(kc, qm, _NT, preferred_element_type=F32)

    def stage_a_bounded(hd, i, slot):
        qs = map_queries(hd, i)
        sums = [jnp.zeros((1, tq), F32), jnp.zeros((1, tq), F32)]
        for st, sz in chunks:
            kc = k_ref[st:st + sz, cols(hd)]
            for m in range(2):
                e = jnp.exp2(scores(kc, qs[m]))
                sums[m] = sums[m] + jnp.sum(e, axis=0, keepdims=True)
                e_scr[slot, m, st:st + sz, :] = e.astype(BF16)
            yield
        c_scr[slot, 0:1, :] = 1.0 / sums[0]
        c_scr[slot, 1:2, :] = lam / sums[1]

    def run(*stages):
        live = list(stages)
        while live:
            live = [g for g in live if next(g, StopIteration) is not StopIteration]

    def stage_a_general(hd, i, slot):
        qs = map_queries(hd, i)
        k = k_ref[:, cols(hd)]
        for m in range(2):
            sm = scores(k, qs[m])
            e = jnp.exp2(sm - jnp.max(sm, axis=0, keepdims=True))
            c_scr[slot, m:m + 1, :] = (lam if m else 1.0) / jnp.sum(e, axis=0, keepdims=True)
            e_scr[slot, m] = e.astype(BF16)

    def stage_b(hd, i, slot):
        c1 = c_scr[slot, 0:1, :].astype(BF16)
        c2 = c_scr[slot, 1:2, :].astype(BF16)
        acc = jnp.zeros((HEAD_W, tq), F32)
        for st, sz in chunks:
            w = e_scr[slot, 0, st:st + sz, :] * c1 - e_scr[slot, 1, st:st + sz, :] * c2
            acc = acc + jnp.dot(vt_scr[hd, :, st:st + sz], w, preferred_element_type=F32)
            yield
        acc_scr[slot] = acc

    def stage_c(hd, i, slot):
        o = acc_scr[slot].T
        ms = jnp.mean(o * o, axis=-1, keepdims=True)
        o = o * lax.rsqrt(ms + EPS) * sg_ref[...] * (1.0 - LAM_INIT)
        o_ref[pl.ds(pl.multiple_of(i * tq, tq), tq), cols(hd)] = o.astype(o_ref.dtype)
        yield

    @pl.when(bounded_ref[0] != 0)
    def _():
        last = n_tiles - 1
        run(stage_a_bounded(0, 0, 0))
        run(stage_a_bounded(0, 1, 1), stage_b(0, 0, 0))
        n_steady = n_tiles - 2
        per_body = ATTN_ROUNDS_PER_BODY
        for hd in range(n_heads):
            def steady(i, parity, hd=hd):
                run(stage_c(hd, i, parity), stage_a_bounded(hd, i + 2, parity), stage_b(hd, i + 1, 1 - parity))

            def body(j, _, steady=steady):
                for r in range(per_body):
                    steady(per_body * j + r, r % 2)
                return 0

            lax.fori_loop(0, n_steady // per_body, body, 0)
            for i in range(n_steady - n_steady % per_body, n_steady):
                steady(i, i % 2)
            if hd + 1 < n_heads:
                run(stage_c(hd, last - 1, 0), stage_a_bounded(hd + 1, 0, 0), stage_b(hd, last, 1))
                run(stage_c(hd, last, 1), stage_a_bounded(hd + 1, 1, 1), stage_b(hd + 1, 0, 0))
            else:
                run(stage_c(hd, last - 1, 0), stage_b(hd, last, 1))
                run(stage_c(hd, last, 1))

    @pl.when(bounded_ref[0] == 0)
    def _():
        for hd in range(n_heads):
            def tile(i, _, hd=hd):
                stage_a_general(hd, i, 0)
                run(stage_b(hd, i, 0))
                run(stage_c(hd, i, 0))
                return 0

            lax.fori_loop(0, n_tiles, tile, 0)


def _attn_call(bounded, q, k, v, lq1, lk1, lq2, lk2, subln_gain):
    bsz, t, d = q.shape
    s = k.shape[1]
    tq = ATTN_ROWS
    hp = ATTN_HEADS_PER_STEP
    assert t % (2 * tq) == 0 and HEADS % hp == 0
    const = lambda b, h: (0, 0)
    head = lambda b, h: (b, 0, h)
    return pl.pallas_call(
        functools.partial(_attn_kernel, tq=tq),
        grid=(bsz, HEADS // hp),
        in_specs=[
            pl.BlockSpec(memory_space=pltpu.SMEM),
            pl.BlockSpec((None, t, hp * HEAD_W), head),
            pl.BlockSpec((None, s, hp * HEAD_W), head),
            pl.BlockSpec((None, s, hp * HEAD_W), head),
            pl.BlockSpec((1, MAP_W), const), pl.BlockSpec((1, MAP_W), const),
            pl.BlockSpec((1, MAP_W), const), pl.BlockSpec((1, MAP_W), const),
            pl.BlockSpec((1, HEAD_W), const),
        ],
        out_specs=pl.BlockSpec((None, t, hp * HEAD_W), head),
        out_shape=jax.ShapeDtypeStruct((bsz, t, d), BF16),
        scratch_shapes=[pltpu.VMEM((2, 2, s, tq), BF16), pltpu.VMEM((2, 8, tq), F32),
                        pltpu.VMEM((2, HEAD_W, tq), F32), pltpu.VMEM((hp, HEAD_W, s), BF16)],
        compiler_params=pltpu.CompilerParams(
            dimension_semantics=("arbitrary", "arbitrary"), vmem_limit_bytes=VMEM_LIMIT),
        name="attn",
    )(bounded, q, k, v, lq1, lk1, lq2, lk2, subln_gain)


def _prefix_matrix(c, rev):
    t = np.arange(c)
    m = (t[None, :] >= t[:, None]) if rev else (t[None, :] <= t[:, None])
    return jnp.asarray(m.astype(np.float32), BF16)


def _split2(g):
    hi = g.astype(BF16)
    mid = (g - hi.astype(F32)).astype(BF16)
    return jnp.concatenate([hi, mid], axis=1)


def _cum_decay(l_ref, g, b_scr, rev):
    c, w = g.shape
    e = jnp.dot(l_ref[...], _split2(g), preferred_element_type=F32)
    b = e[:, 0:w] + e[:, w:2 * w]
    b_scr[...] = b
    tot = b_scr[0:1, :] if rev else b_scr[c - 1:c, :]
    return b, tot


def _state_update(st, v, kk, e_to_end, e_total):
    kd = (kk * e_to_end).astype(BF16)
    vt = v.astype(F32).T.astype(BF16)
    return st * e_total + jnp.dot(vt, kd, preferred_element_type=F32)


def _ctx_pair(st_f, st_b, gf, vf, gb, vb, lf_ref, lb_ref, bf_scr, bb_scr):
    bf, tot_f = _cum_decay(lf_ref, gf, bf_scr, False)
    bb, tot_b = _cum_decay(lb_ref, gb, bb_scr, True)
    st_f = _state_update(st_f, vf, 1.0 - jnp.exp2(gf), jnp.exp2(tot_f - bf), jnp.exp2(tot_f))
    st_b = _state_update(st_b, vb, 1.0 - jnp.exp2(gb), jnp.exp2(tot_b - bb), jnp.exp2(tot_b))
    return st_f, st_b


def _level_operand(m, ch):
    qf, kk, eg, b, b_scr, rev = ch["qf"], ch["kk"], ch["eg"], ch["b"], ch["b_scr"], ch["rev"]
    c, w = b.shape
    ref = m if rev else m - 1
    bcast = lambda r, n: jnp.broadcast_to(b_scr[r:r + 1, :], (n, w))
    if m >= 8:
        narrow = m % 16 == 0
        q_, k_ = (ch["qb"], ch["kb"]) if narrow else (qf, kk)
        decay = (lambda e: jnp.exp2(e).astype(BF16)) if narrow else jnp.exp2
        pieces = []
        for j in range(c // (2 * m)):
            lo = slice(j * 2 * m, j * 2 * m + m)
            hi = slice(j * 2 * m + m, (j + 1) * 2 * m)
            r = bcast(j * 2 * m + ref, m)
            if rev:
                pieces += [q_[lo] * decay(b[lo] - r), k_[hi] * decay(r - b[hi])]
            else:
                pieces += [k_[lo] * decay(r - b[lo]), q_[hi] * decay(b[hi] - r)]
        return jnp.concatenate(pieces, axis=0).astype(BF16)
    row = lax.broadcasted_iota(jnp.int32, (c, w), 0)
    upper = (row & m) != 0
    q_rows = jnp.logical_not(upper) if rev else upper
    if m == 1:
        f = jnp.where(q_rows, eg, 1.0)
    else:
        if m == 4:
            r = jnp.concatenate([bcast(8 * j + ref, 8) for j in range(c // 8)], axis=0)
        else:
            sub = lax.broadcasted_iota(jnp.int32, (8, w), 0)
            r = jnp.concatenate([jnp.where(sub < 4, bcast(8 * j + ref, 8), bcast(8 * j + 4 + ref, 8))
                                 for j in range(c // 8)], axis=0)
        f = jnp.exp2(-jnp.abs(b - r))
    return (jnp.where(q_rows, qf, kk) * f).astype(BF16)


def _mid_operands(ch):
    b, b_scr = ch["b"], ch["b_scr"]
    c, w = b.shape
    mid = MID_BLOCK // 2
    r = jnp.concatenate([jnp.broadcast_to(b_scr[j * MID_BLOCK + mid:j * MID_BLOCK + mid + 1, :], (MID_BLOCK, w))
                         for j in range(c // MID_BLOCK)], axis=0)
    d = b - r
    return ch["qb"] * jnp.exp2(d).astype(BF16), ch["kb"] * jnp.exp2(-d).astype(BF16)


def _chunk_decays(ch):
    b, tot = _cum_decay(ch["l_ref"], ch["g"], ch["b_scr"], ch["rev"])
    eg = jnp.exp2(ch["g"])
    kk = 1.0 - eg
    ch.update(b=b, tot=tot, eg=eg, kk=kk, kb=kk.astype(BF16), qb=ch["q"], qf=ch["q"].astype(F32))


def _chunk_state(st, ch, with_diagonal):
    b, tot, kk, qf, v = ch["b"], ch["tot"], ch["kk"], ch["qf"], ch["v"]
    st_new = _state_update(st, v, kk, jnp.exp2(tot - b), jnp.exp2(tot))
    qd = (qf * jnp.exp2(b)).astype(BF16)
    o = lax.dot_general(qd, st.astype(BF16), _NT, preferred_element_type=F32)
    if with_diagonal:
        o = o + jnp.sum(qf * kk, axis=-1, keepdims=True) * v.astype(F32)
    ch["o"] = o
    return st_new


def _chunk_level(ch, m, differ):
    rev = ch["rev"]
    c = ch["b"].shape[0]
    h = c // 2
    if m == "mid":
        xq, xk = _mid_operands(ch)
        block = MID_BLOCK
    else:
        xq = xk = _level_operand(m, ch)
        block = 2 * m
    quadrant = lambda rows, cols_: lax.dot_general(xq[rows], xk[cols_], _NT, preferred_element_type=F32)
    lo, hi = slice(0, h), slice(h, c)
    if block == c:
        ch["a_off"] = (quadrant(lo, hi) if rev else quadrant(hi, lo)).astype(BF16)
    elif block == h:
        ch["a0"], ch["a1"] = quadrant(lo, lo), quadrant(hi, hi)
    else:
        same_block = differ < block
        ch["a0"] = jnp.where(same_block, quadrant(lo, lo), ch["a0"])
        ch["a1"] = jnp.where(same_block, quadrant(hi, hi), ch["a1"])


def _chunk_output(ch, valid):
    v, rev = ch["v"], ch["rev"]
    c = v.shape[0]
    h = c // 2
    a0 = jnp.where(valid, ch["a0"], 0.0).astype(BF16)
    a1 = jnp.where(valid, ch["a1"], 0.0).astype(BF16)
    if rev:
        o_lo = jnp.dot(jnp.concatenate([a0, ch["a_off"]], axis=1), v, preferred_element_type=F32)
        o_hi = jnp.dot(a1, v[h:c], preferred_element_type=F32)
    else:
        o_lo = jnp.dot(a0, v[0:h], preferred_element_type=F32)
        o_hi = jnp.dot(jnp.concatenate([ch["a_off"], a1], axis=1), v, preferred_element_type=F32)
    return ch["o"] + jnp.concatenate([o_lo, o_hi], axis=0)


def _lat_group(chunks, st_f, st_b, mid_ok):
    c = chunks[0]["g"].shape[0]
    h = c // 2
    for ch in chunks:
        _chunk_decays(ch)
    for ch in chunks:
        if ch["rev"]:
            st_b = _chunk_state(st_b, ch, not mid_ok)
        else:
            st_f = _chunk_state(st_f, ch, not mid_ok)
    ti = lax.broadcasted_iota(jnp.int32, (h, h), 0)
    si = lax.broadcasted_iota(jnp.int32, (h, h), 1)
    differ = ti ^ si
    first_lvl = int(math.log2(MID_BLOCK)) if mid_ok else 0
    levels = [1 << lvl for lvl in reversed(range(first_lvl, int(math.log2(c))))] + (["mid"] if mid_ok else [])
    for m in levels:
        for ch in chunks:
            _chunk_level(ch, m, differ)
    if mid_ok:
        valid_f, valid_b = ti >= si, ti <= si
    else:
        valid_f, valid_b = ti > si, ti < si
    outs = [_chunk_output(ch, valid_b if ch["rev"] else valid_f) for ch in chunks]
    return st_f, st_b, outs


def _hgrn_kernel(mid_ref, q_ref, v_ref, gf_ref, gb_ref, lf_ref, lb_ref, hg_ref, o_ref, of_scr, ob_scr, bf_scr,
                 bb_scr, *, lc, c):
    t = q_ref.shape[0]
    dk = gf_ref.shape[1]
    dv = v_ref.shape[1]
    n_ctx = lc // c
    n_lat = t // c
    st0 = jnp.zeros((dv, dk), F32)

    def ctx(j, carry):
        st_f, st_b = carry
        rf = pl.multiple_of(j * c, c)
        rb = pl.multiple_of((n_ctx - 1 - j) * c, c)
        return _ctx_pair(st_f, st_b, gf_ref[pl.ds(rf, c), :], v_ref[pl.ds(rf, c), :], gb_ref[pl.ds(rb, c), :],
                         v_ref[pl.ds(rb, c), :], lf_ref, lb_ref, bf_scr.at[0], bb_scr.at[0])

    carry = lax.fori_loop(0, n_ctx, ctx, (st0, st0))

    def latents(mid_ok):
        def lat(jj, carry):
            st_f, st_b = carry
            chunks = []
            for u in range(LAT_UNROLL):
                j = jj * LAT_UNROLL + u
                rf = pl.multiple_of(j * c, c)
                rb = pl.multiple_of((n_lat - 1 - j) * c, c)
                sf = pl.multiple_of(lc + j * c, c)
                sb = pl.multiple_of(lc + (n_lat - 1 - j) * c, c)
                chunks.append(dict(g=gf_ref[pl.ds(sf, c), :], q=q_ref[pl.ds(rf, c), :], v=v_ref[pl.ds(sf, c), :],
                                   l_ref=lf_ref, b_scr=bf_scr.at[u], rev=False, out=of_scr, row=rf))
                chunks.append(dict(g=gb_ref[pl.ds(sb, c), :], q=q_ref[pl.ds(rb, c), :], v=v_ref[pl.ds(sb, c), :],
                                   l_ref=lb_ref, b_scr=bb_scr.at[u], rev=True, out=ob_scr, row=rb))
            st_f, st_b, outs = _lat_group(chunks, st_f, st_b, mid_ok)
            for ch, o in zip(chunks, outs):
                ch["out"][pl.ds(ch["row"], c), :] = o
            return st_f, st_b

        lax.fori_loop(0, n_lat // LAT_UNROLL, lat, carry)

    @pl.when(mid_ref[0] != 0)
    def _():
        latents(True)

    @pl.when(mid_ref[0] == 0)
    def _():
        latents(False)

    o = of_scr[...] + ob_scr[...]
    ms = jnp.mean(o * o, axis=-1, keepdims=True)
    o_ref[...] = (o * lax.rsqrt(ms + EPS) * hg_ref[...]).astype(o_ref.dtype)


def _hgrn_call(mid_ok, hq, hv, gf, gb, hgrn_gain, lc):
    bsz, t, d = hq.shape
    s = hv.shape[1]
    c = CHUNK
    assert lc % c == 0 and t % c == 0
    lf = _prefix_matrix(c, False)
    lb = _prefix_matrix(c, True)
    const = lambda b, h: (0, 0)
    kern = functools.partial(_hgrn_kernel, lc=lc, c=c)
    return pl.pallas_call(
        kern,
        grid=(bsz, HEADS),
        in_specs=[
            pl.BlockSpec(memory_space=pltpu.SMEM),
            pl.BlockSpec((None, t, HEAD_W), lambda b, h: (b, 0, h)),
            pl.BlockSpec((None, s, HEAD_W), lambda b, h: (b, 0, h)),
            pl.BlockSpec((None, s, HEAD_W), lambda b, h: (b, 0, h)),
            pl.BlockSpec((None, s, HEAD_W), lambda b, h: (b, 0, h)),
            pl.BlockSpec(lf.shape, const), pl.BlockSpec(lb.shape, const),
            pl.BlockSpec((1, HEAD_W), const),
        ],
        out_specs=pl.BlockSpec((None, t, HEAD_W), lambda b, h: (b, 0, h)),
        out_shape=jax.ShapeDtypeStruct((bsz, t, d), BF16),
        scratch_shapes=[pltpu.VMEM((t, HEAD_W), F32), pltpu.VMEM((t, HEAD_W), F32),
                        pltpu.VMEM((LAT_UNROLL, c, HEAD_W), F32), pltpu.VMEM((LAT_UNROLL, c, HEAD_W), F32)],
        compiler_params=pltpu.CompilerParams(
            dimension_semantics=("arbitrary", "arbitrary"), vmem_limit_bytes=VMEM_LIMIT),
        name="hgrn",
    )(mid_ok, hq, hv, gf, gb, lf, lb, hgrn_gain)


def _merge_kernel(oa_ref, oh_ref, h_ref, x_ref, mod_ref, wza_ref, wzh_ref, wga_ref, wgh_ref, wba_ref, wbh_ref,
                  wo_ref, out_ref, *, d):
    b = pl.program_id(0)
    h = h_ref[...]
    gate_path = lambda w_ref: jnp.dot(h, w_ref[...], preferred_element_type=F32)
    silu = lambda a: a * _gate_sigmoid(a)
    ya = oa_ref[...] * silu(gate_path(wza_ref)).astype(BF16)
    yh = oh_ref[...] * silu(gate_path(wzh_ref)).astype(BF16)
    ta = jnp.dot(ya, wba_ref[...], preferred_element_type=F32)
    th = jnp.dot(yh, wbh_ref[...], preferred_element_type=F32)
    mix = _gate_sigmoid(gate_path(wga_ref)) * ta + _gate_sigmoid(gate_path(wgh_ref)) * th
    y = jnp.dot(mix.astype(BF16), wo_ref[...], preferred_element_type=F32)
    gate = mod_ref[pl.ds(b, 1), 2 * d:3 * d]
    out_ref[...] = x_ref[...] + gate * y


def _merge_call(oa, oh, h, x, mod, w_in, wba, wbh, wo):
    bsz, t, d = x.shape
    tm = MERGE_ROWS
    tok = lambda b, i: (b, i, 0)
    const = lambda b, i: (0, 0)
    resident = lambda idx=const: pl.BlockSpec((d, d), idx, pipeline_mode=pl.Buffered(1))
    in_group = lambda g: resident(lambda b, i: (0, g))
    return pl.pallas_call(
        functools.partial(_merge_kernel, d=d),
        grid=(bsz, t // tm),
        in_specs=[
            pl.BlockSpec((None, tm, d), tok), pl.BlockSpec((None, tm, d), tok),
            pl.BlockSpec((None, tm, d), tok), pl.BlockSpec((None, tm, d), tok),
            pl.BlockSpec(mod.shape, const),
            in_group(G_ATTN_Z), in_group(G_HGRN_Z), in_group(G_MERGE), in_group(G_MERGE + 1),
            resident(), resident(), resident(),
        ],
        out_specs=pl.BlockSpec((None, tm, d), tok),
        out_shape=jax.ShapeDtypeStruct((bsz, t, d), F32),
        compiler_params=pltpu.CompilerParams(
            dimension_semantics=("arbitrary", "arbitrary"), vmem_limit_bytes=VMEM_LIMIT),
        name="merge",
    )(oa, oh, h, x, mod, w_in, w_in, w_in, w_in, wba, wbh, wo)


def _rope_tables(t, lc):
    rows = t // GRID_W
    row = jnp.repeat(jnp.arange(rows, dtype=F32), GRID_W)
    col = jnp.tile(jnp.arange(GRID_W, dtype=F32), rows)
    half = MAP_W // 2
    inv = ROPE_BASE ** (-jnp.arange(0, half, 2, dtype=F32) / half)
    ar = row[:, None] * inv[None, :]
    ac = col[:, None] * inv[None, :]
    ang = jnp.concatenate([ar, ar, ac, ac], axis=-1)
    ang = jnp.concatenate([ang, ang], axis=-1)
    cos, sin = jnp.cos(ang), jnp.sin(ang)
    first = (jnp.arange(HEAD_W) % (half)) < (half // 2)
    sin_a = jnp.where(first[None, :], -sin, 0.0)
    sin_b = jnp.where(first[None, :], 0.0, sin)
    pad = lambda a, v: jnp.concatenate([jnp.full((lc, HEAD_W), v, F32), a], axis=0)
    return pad(cos, 1.0), pad(sin_a, 0.0), pad(sin_b, 0.0)


def kernel(x, c, ctx, c_ctx, w_mod, b_mod, norm_gain, w_in, q_norm_gain, k_norm_gain, lambda_q1, lambda_k1,
           lambda_q2, lambda_k2, subln_gain, hgrn_lb_fwd, hgrn_lb_bwd, hgrn_norm_gain, w_br_attn, w_br_hgrn,
           w_out):
    bsz, t, d = x.shape
    lc = ctx.shape[1]
    assert w_mod.shape[0] == 1, "single-layer problem"

    mod_rows = ((bsz + 1 + 7) // 8) * 8
    cc = jnp.concatenate([c, c_ctx[None, :], jnp.zeros((mod_rows - bsz - 1, d), F32)], axis=0)
    mod = _mod_call(cc, w_mod[0], b_mod[0][None, :])

    cos_t, sin_a, sin_b = _rope_tables(t, lc)
    q_gain = jnp.tile(q_norm_gain[0], 2)[None, :] * (MAP_W ** -0.5 * LOG2E)
    k_gain = jnp.tile(k_norm_gain[0], 2)[None, :]
    score_bound = (1.03 * MAP_W) * jnp.max(jnp.abs(q_gain)) * jnp.max(jnp.abs(k_gain))
    bounded = (score_bound <= MAX_SOFTMAX_SHIFT).astype(jnp.int32).reshape(1)
    lane = np.arange(256)
    gsum = jnp.asarray((lane[:, None] // MAP_W) == (lane[None, :] // MAP_W), BF16)
    w_bf = w_in[0].astype(BF16)
    k_a, v_a, v_h, g_f, g_b, q_a, q_h, h_lat = _proj_call(
        x, ctx, mod, norm_gain, w_bf, q_gain, k_gain, cos_t, sin_a, sin_b, hgrn_lb_fwd, hgrn_lb_bwd, gsum)

    o_a = _attn_call(bounded, q_a, k_a, v_a, lambda_q1, lambda_k1, lambda_q2, lambda_k2, subln_gain)
    lb_min = jnp.minimum(jnp.min(jax.nn.softmax(hgrn_lb_fwd.astype(F32), axis=0)[0]),
                         jnp.min(jax.nn.softmax(hgrn_lb_bwd.astype(F32), axis=0)[0]))
    mid_ok = (-jnp.log2(lb_min) * (MID_BLOCK // 2) <= MID_MAX_EXPONENT).astype(jnp.int32).reshape(1)
    o_h = _hgrn_call(mid_ok, q_h, v_h, g_f, g_b, hgrn_norm_gain, lc)
    return _merge_call(o_a, o_h, h_lat, x, mod, w_bf,
                       w_br_attn[0].astype(BF16), w_br_hgrn[0].astype(BF16), w_out[0].astype(BF16))
```
